```python
import math
import jax, jax.numpy as jnp
from jax import lax
import numpy as np

D_MODEL = 1024
BATCH = 16
SEQ = 2048
DEPTH = 4

GRID_W = 64
CTX_LEN = 256
HEAD_DIM = 64
Q_BLOCK = 128
WINDOW = 128
ROPE_THETA = 10000.0
EPS = 1e-6
NEG_INF = -1e30

A_Q_HEADS = 8
A_KV_HEADS = 2
A_GROUP = A_Q_HEADS // A_KV_HEADS
B_HEADS = 8
B_Q_RANK = 256
B_KV_RANK = 128
B_NOPE = 64
B_ROPE = 32
B_V = 64
C_Q_HEADS = 8
C_KV_HEADS = 2
C_GROUP = C_Q_HEADS // C_KV_HEADS
D_HEADS = 4
D_V = 2 * HEAD_DIM

EVEN_SIZES = (A_Q_HEADS * HEAD_DIM, A_KV_HEADS * HEAD_DIM, A_KV_HEADS * HEAD_DIM, B_Q_RANK, B_KV_RANK, B_ROPE)
ODD_SIZES = (C_Q_HEADS * HEAD_DIM, C_KV_HEADS * HEAD_DIM, C_KV_HEADS * HEAD_DIM,
             D_HEADS * 2 * HEAD_DIM, D_HEADS * 2 * HEAD_DIM, D_HEADS * D_V)
EVEN_IN = sum(EVEN_SIZES)
ODD_IN = sum(ODD_SIZES)
MIX_WIDTH = A_Q_HEADS * HEAD_DIM + B_HEADS * B_V
FFN_HIDDEN = -(-8 * D_MODEL // (3 * 256)) * 256
N_EVEN = (DEPTH + 1) // 2
N_ODD = DEPTH // 2

kernel_name = "hybrid_diffusion_prefix_trunk"


def rms_norm(x, g):
    xf = x.astype(jnp.float32)
    y = xf * lax.rsqrt(jnp.mean(xf * xf, axis=-1, keepdims=True) + EPS)
    return (y * g.astype(jnp.float32)).astype(x.dtype)


def split_cols(p, sizes):
    return jnp.split(p, [int(v) for v in np.cumsum(sizes)[:-1]], axis=-1)


def rope_1d(x, pos):
    half = x.shape[-1] // 2
    freqs = ROPE_THETA ** (-jnp.arange(half, dtype=jnp.float32) / half)
    ang = pos.astype(jnp.float32)[:, None] * freqs[None, :]
    ang = ang.reshape((ang.shape[0],) + (1,) * (x.ndim - 3) + (half,))
    cos, sin = jnp.cos(ang), jnp.sin(ang)
    x1 = x[..., :half].astype(jnp.float32)
    x2 = x[..., half:].astype(jnp.float32)
    return jnp.concatenate([x1 * cos - x2 * sin, x2 * cos + x1 * sin], axis=-1).astype(x.dtype)


def rope_2d(x, rows, cols):
    h = x.shape[-1] // 2
    return jnp.concatenate([rope_1d(x[..., :h], rows), rope_1d(x[..., h:], cols)], axis=-1)


def sweep_query_blocks(fn, q):
    B, S = q.shape[:2]
    nb = S // Q_BLOCK
    qb = jnp.moveaxis(q.reshape((B, nb, Q_BLOCK) + q.shape[2:]), 1, 0)
    ob = lax.map(fn, qb)
    return jnp.moveaxis(ob, 0, 1).reshape((B, S) + ob.shape[3:])


def gqa_attend(q, k, v, scale):
    s = jnp.einsum('bqhgd,bkhd->bhgqk', q, k, preferred_element_type=jnp.float32) * scale
    p = jax.nn.softmax(s, axis=-1)
    return jnp.einsum('bhgqk,bkhd->bqhgd', p.astype(v.dtype), v)


def sink_attend(q, k, v, sink):
    Hk, G, d = q.shape[2], q.shape[3], q.shape[4]
    s = jnp.einsum('bqhgd,bkhd->bhgqk', q, k, preferred_element_type=jnp.float32) * d ** -0.5
    sk = jnp.broadcast_to(sink.astype(jnp.float32).reshape(Hk, G, 1, 1), s.shape[:-1] + (1,))
    p = jax.nn.softmax(jnp.concatenate([s, sk], axis=-1), axis=-1)[..., :-1]
    return jnp.einsum('bhgqk,bkhd->bqhgd', p.astype(v.dtype), v)


def windowed_sink_attention(q, k, v, k_ctx, v_ctx, sink):
    B, S, Hk, G, d = q.shape
    nb = S // Q_BLOCK
    L = 3 * Q_BLOCK
    scale = d ** -0.5
    qb = q.reshape(B, nb, Q_BLOCK, Hk, G, d)

    def band(t):
        tb = t.reshape(B, nb, Q_BLOCK, Hk, t.shape[-1])
        tp = jnp.pad(tb, ((0, 0), (1, 1), (0, 0), (0, 0), (0, 0)))
        return jnp.concatenate([tp[:, :-2], tp[:, 1:-1], tp[:, 2:]], axis=2)

    kw, vw = band(k), band(v)
    blk = jnp.arange(nb)[:, None, None]
    q_pos = blk * Q_BLOCK + jnp.arange(Q_BLOCK)[None, :, None]
    k_pos = (blk - 1) * Q_BLOCK + jnp.arange(L)[None, None, :]
    valid = (jnp.abs(q_pos - k_pos) <= WINDOW) & (k_pos >= 0) & (k_pos < S)
    s_loc = jnp.einsum('bnqhgd,bnkhd->bnhgqk', qb, kw, preferred_element_type=jnp.float32) * scale
    s_loc = jnp.where(valid[None, :, None, None, :, :], s_loc, NEG_INF)
    s_ctx = jnp.einsum('bnqhgd,bchd->bnhgqc', qb, k_ctx, preferred_element_type=jnp.float32) * scale
    sk = jnp.broadcast_to(sink.astype(jnp.float32).reshape(Hk, G, 1, 1), s_loc.shape[:-1] + (1,))
    p = jax.nn.softmax(jnp.concatenate([s_loc, s_ctx, sk], axis=-1), axis=-1)
    C = k_ctx.shape[1]
    o = (jnp.einsum('bnhgqk,bnkhd->bnqhgd', p[..., :L].astype(v.dtype), vw)
         + jnp.einsum('bnhgqc,bchd->bnqhgd', p[..., L:L + C].astype(v.dtype), v_ctx))
    return o.reshape(B, S, Hk, G, v.shape[-1])


def diff_attend(q, k, v, lam):
    s = jnp.einsum('bqhid,bkhid->bhiqk', q, k, preferred_element_type=jnp.float32) * q.shape[-1] ** -0.5
    p = jax.nn.softmax(s, axis=-1)
    a = p[:, :, 0] - lam * p[:, :, 1]
    return jnp.einsum('bhqk,bkhd->bqhd', a.astype(v.dtype), v)


def even_project(t, rope, w_in, q_norm, w_uq, kv_norm, w_ukv):
    B, L, _ = t.shape
    qa, ka, va, cq, ckv, kpe = split_cols(t @ w_in, EVEN_SIZES)
    qa = rope(qa.reshape(B, L, A_KV_HEADS, A_GROUP, HEAD_DIM))
    ka = rope(ka.reshape(B, L, A_KV_HEADS, HEAD_DIM))
    va = va.reshape(B, L, A_KV_HEADS, HEAD_DIM)
    qb = (rms_norm(cq, q_norm) @ w_uq).reshape(B, L, B_HEADS, B_NOPE + B_ROPE)
    qb = jnp.concatenate([qb[..., :B_NOPE], rope(qb[..., B_NOPE:])], axis=-1)[:, :, :, None, :]
    kvb = (rms_norm(ckv, kv_norm) @ w_ukv).reshape(B, L, B_HEADS, B_NOPE + B_V)
    kpe = jnp.broadcast_to(rope(kpe[:, :, None, :]), (B, L, B_HEADS, B_ROPE))
    kb = jnp.concatenate([kvb[..., :B_NOPE], kpe], axis=-1)
    vb = kvb[..., B_NOPE:]
    return qa, ka, va, qb, kb, vb


def mixer_even(h, hc, rows, cols, w_in, sink, q_norm, w_uq, kv_norm, w_ukv, w_out, need_ctx):
    B, S = h.shape[:2]
    qa, ka, va, qb, kb, vb = even_project(h, lambda t: rope_2d(t, rows, cols), w_in, q_norm, w_uq, kv_norm, w_ukv)
    qa_c, ka_c, va_c, qb_c, kb_c, vb_c = even_project(hc, lambda t: t, w_in, q_norm, w_uq, kv_norm, w_ukv)
    scale_b = (B_NOPE + B_ROPE) ** -0.5
    oa = windowed_sink_attention(qa, ka, va, ka_c, va_c, sink)
    kb_all = jnp.concatenate([kb_c, kb], axis=1)
    vb_all = jnp.concatenate([vb_c, vb], axis=1)
    ob = sweep_query_blocks(lambda qq: gqa_attend(qq, kb_all, vb_all, scale_b), qb)
    y = jnp.concatenate([oa.reshape(B, S, -1), ob.reshape(B, S, -1)], axis=-1) @ w_out
    yc = None
    if need_ctx:
        C = hc.shape[1]
        oa_c = sink_attend(qa_c, ka_c, va_c, sink)
        ob_c = gqa_attend(qb_c, kb_c, vb_c, scale_b)
        yc = jnp.concatenate([oa_c.reshape(B, C, -1), ob_c.reshape(B, C, -1)], axis=-1) @ w_out
    return y, yc


def odd_project(t, rope, w_in, qk_norm):
    B, L, _ = t.shape
    qc, kc, vc, qd, kd, vd = split_cols(t @ w_in, ODD_SIZES)
    qc = rope(rms_norm(qc.reshape(B, L, C_KV_HEADS, C_GROUP, HEAD_DIM), qk_norm[0]))
    kc = rope(rms_norm(kc.reshape(B, L, C_KV_HEADS, HEAD_DIM), qk_norm[1]))
    vc = vc.reshape(B, L, C_KV_HEADS, HEAD_DIM)
    qd = rope(qd.reshape(B, L, D_HEADS, 2, HEAD_DIM))
    kd = rope(kd.reshape(B, L, D_HEADS, 2, HEAD_DIM))
    vd = vd.reshape(B, L, D_HEADS, D_V)
    return qc, kc, vc, qd, kd, vd


def mixer_odd(h, hc, rows, cols, w_in, qk_norm, lam_p, subln, w_out, lam_init, need_ctx):
    B, S = h.shape[:2]
    qc, kc, vc, qd, kd, vd = odd_project(h, lambda t: rope_2d(t, rows, cols), w_in, qk_norm)
    qc_c, kc_c, vc_c, qd_c, kd_c, vd_c = odd_project(hc, lambda t: t, w_in, qk_norm)
    lp = lam_p.astype(jnp.float32)
    lam = jnp.exp(jnp.sum(lp[0] * lp[1])) - jnp.exp(jnp.sum(lp[2] * lp[3])) + lam_init
    scale_c = HEAD_DIM ** -0.5
    kc_all = jnp.concatenate([kc_c, kc], axis=1)
    vc_all = jnp.concatenate([vc_c, vc], axis=1)
    kd_all = jnp.concatenate([kd_c, kd], axis=1)
    vd_all = jnp.concatenate([vd_c, vd], axis=1)
    oc = sweep_query_blocks(lambda qq: gqa_attend(qq, kc_all, vc_all, scale_c), qc)
    od = sweep_query_blocks(lambda qq: diff_attend(qq, kd_all, vd_all, lam), qd)
    od = rms_norm(od, subln) * (1.0 - lam_init)
    y = jnp.concatenate([oc.reshape(B, S, -1), od.reshape(B, S, -1)], axis=-1) @ w_out
    yc = None
    if need_ctx:
        C = hc.shape[1]
        oc_c = gqa_attend(qc_c, kc_c, vc_c, scale_c)
        od_c = rms_norm(diff_attend(qd_c, kd_c, vd_c, lam), subln) * (1.0 - lam_init)
        yc = jnp.concatenate([oc_c.reshape(B, C, -1), od_c.reshape(B, C, -1)], axis=-1) @ w_out
    return y, yc


def swiglu(h, w_in, w_out):
    g, u = jnp.split(h @ w_in, 2, axis=-1)
    return (jax.nn.silu(g) * u) @ w_out


def diff_lambda_init(layer):
    return 0.8 - 0.6 * math.exp(-0.3 * layer)


def setup_inputs(seed: int = 0) -> dict:
    key = jax.random.key(seed)
    ks = jax.random.split(key, 24)
    D = D_MODEL

    def nrm(k, shape, scale):
        return jax.random.normal(k, shape, jnp.float32) * scale

    return {
        "x": nrm(ks[0], (BATCH, SEQ, D), 1.0),
        "c": nrm(ks[1], (BATCH, D), 1.0),
        "ctx": nrm(ks[2], (BATCH, CTX_LEN, D), 1.0),
        "c_ctx": nrm(ks[3], (D,), 1.0),
        "ada_w": nrm(ks[4], (DEPTH, D, 6 * D), 0.5 * D ** -0.5),
        "ada_b": nrm(ks[5], (DEPTH, 6 * D), 0.02),
        "norm_g": 1.0 + nrm(ks[6], (DEPTH, 4, D), 0.1),
        "ffn_w_in": nrm(ks[7], (DEPTH, D, 2 * FFN_HIDDEN), D ** -0.5),
        "ffn_w_out": nrm(ks[8], (DEPTH, FFN_HIDDEN, D), FFN_HIDDEN ** -0.5),
        "ev_w_in": nrm(ks[9], (N_EVEN, D, EVEN_IN), D ** -0.5),
        "ev_sink": nrm(ks[10], (N_EVEN, A_Q_HEADS), 0.5),
        "ev_q_norm": 1.0 + nrm(ks[11], (N_EVEN, B_Q_RANK), 0.1),
        "ev_w_uq": nrm(ks[12], (N_EVEN, B_Q_RANK, B_HEADS * (B_NOPE + B_ROPE)), B_Q_RANK ** -0.5),
        "ev_kv_norm": 1.0 + nrm(ks[13], (N_EVEN, B_KV_RANK), 0.1),
        "ev_w_ukv": nrm(ks[14], (N_EVEN, B_KV_RANK, B_HEADS * (B_NOPE + B_V)), B_KV_RANK ** -0.5),
        "ev_w_out": nrm(ks[15], (N_EVEN, MIX_WIDTH, D), MIX_WIDTH ** -0.5),
        "od_w_in": nrm(ks[16], (N_ODD, D, ODD_IN), D ** -0.5),
        "od_qk_norm": 1.0 + nrm(ks[17], (N_ODD, 2, HEAD_DIM), 0.1),
        "od_lambda": nrm(ks[18], (N_ODD, 4, HEAD_DIM), 0.1),
        "od_subln": 1.0 + nrm(ks[19], (N_ODD, D_V), 0.1),
        "od_w_out": nrm(ks[20], (N_ODD, MIX_WIDTH, D), MIX_WIDTH ** -0.5),
    }


def reference(x, c, ctx, c_ctx, ada_w, ada_b, norm_g, ffn_w_in, ffn_w_out,
              ev_w_in, ev_sink, ev_q_norm, ev_w_uq, ev_kv_norm, ev_w_ukv, ev_w_out,
              od_w_in, od_qk_norm, od_lambda, od_subln, od_w_out):
    B, S, _ = x.shape
    ROWS = S // GRID_W
    rows = jnp.repeat(jnp.arange(ROWS), GRID_W)
    cols = jnp.tile(jnp.arange(GRID_W), ROWS)
    s_c = jax.nn.silu(c)
    s_cc = jax.nn.silu(c_ctx)
    xc = ctx
    for l in range(DEPTH):
        need_ctx = l < DEPTH - 1
        mod = (s_c @ ada_w[l] + ada_b[l])[:, None, :]
        mod_c = s_cc @ ada_w[l] + ada_b[l]
        sh1, sc1, g1, sh2, sc2, g2 = jnp.split(mod, 6, axis=-1)
        csh1, csc1, cg1, csh2, csc2, cg2 = jnp.split(mod_c, 6, axis=-1)
        h = rms_norm(x, norm_g[l, 0]) * (1.0 + sc1) + sh1
        hc = rms_norm(xc, norm_g[l, 0]) * (1.0 + csc1) + csh1
        if l % 2 == 0:
            e = l // 2
            y, yc = mixer_even(h, hc, rows, cols, ev_w_in[e], ev_sink[e], ev_q_norm[e], ev_w_uq[e],
                               ev_kv_norm[e], ev_w_ukv[e], ev_w_out[e], need_ctx)
        else:
            o = l // 2
            y, yc = mixer_odd(h, hc, rows, cols, od_w_in[o], od_qk_norm[o], od_lambda[o], od_subln[o],
                              od_w_out[o], diff_lambda_init(l), need_ctx)
        x = x + g1 * rms_norm(y, norm_g[l, 1])
        h = rms_norm(x, norm_g[l, 2]) * (1.0 + sc2) + sh2
        x = x + g2 * rms_norm(swiglu(h, ffn_w_in[l], ffn_w_out[l]), norm_g[l, 3])
        if need_ctx:
            xc = xc + cg1 * rms_norm(yc, norm_g[l, 1])
            hc = rms_norm(xc, norm_g[l, 2]) * (1.0 + csc2) + csh2
            xc = xc + cg2 * rms_norm(swiglu(hc, ffn_w_in[l], ffn_w_out[l]), norm_g[l, 3])
    return x
```

```python
import functools
import math

import jax
import jax.numpy as jnp
from jax import lax
from jax.experimental import pallas as pl
from jax.experimental.pallas import tpu as pltpu

F32 = jnp.float32
BF16 = jnp.bfloat16

D_MODEL = 1024
DEPTH = 4
GRID_W = 64
HEAD_DIM = 64
WINDOW = 128
ROPE_THETA = 10000.0
EPS = 1e-6
NEG_INF = -1e30

A_Q_HEADS = 8
B_HEADS = 8
B_Q_RANK = 256
B_KV_RANK = 128
B_NOPE = 64
B_ROPE = 32
B_V = 64
D_HEADS = 4
FFN_HIDDEN = 2816

LANES = 128
TOK = 256
FFN_CHUNK = 256

SCALE_64 = HEAD_DIM ** -0.5
SCALE_B = (B_NOPE + B_ROPE) ** -0.5


def _rms(x, gain):
    ms = jnp.mean(x * x, axis=-1, keepdims=True)
    return x * lax.rsqrt(ms + EPS) * gain


def _lane(shape):
    return lax.broadcasted_iota(jnp.int32, shape, 1)


def _rope(x, cos, sin_signed, half):
    first = (_lane(x.shape) % (2 * half)) < half
    partner = jnp.where(first, pltpu.roll(x, LANES - half, 1), pltpu.roll(x, half, 1))
    return x * cos + partner * sin_signed


def _head_norm(x, gain):
    lo = _lane(x.shape) < HEAD_DIM
    x2 = x * x
    s_lo = jnp.sum(jnp.where(lo, x2, 0.0), axis=-1, keepdims=True)
    s_hi = jnp.sum(jnp.where(lo, 0.0, x2), axis=-1, keepdims=True)
    inv = jnp.where(lo, lax.rsqrt(s_lo / HEAD_DIM + EPS), lax.rsqrt(s_hi / HEAD_DIM + EPS))
    return x * inv * gain


def _tile(ref, j, n=1):
    return ref[:, j * LANES:(j + n) * LANES]


def _mod_norm(x, gain, shift, scale):
    return _rms(x, gain) * (1.0 + scale) + shift


def _ada_kernel(s_ref, w_ref, b_ref, o_ref):
    s = s_ref[...]
    a = (s * jax.nn.sigmoid(s)).astype(BF16)
    o_ref[...] = jnp.dot(a, w_ref[...].astype(BF16), preferred_element_type=F32) + b_ref[...]


def _ada(s_rows, ada_w, ada_b):
    rows = s_rows.shape[0]
    nblk = (6 * D_MODEL) // D_MODEL
    return pl.pallas_call(
        _ada_kernel,
        out_shape=jax.ShapeDtypeStruct((DEPTH, rows, 6 * D_MODEL), F32),
        grid=(DEPTH, nblk),
        in_specs=[
            pl.BlockSpec((rows, D_MODEL), lambda l, n: (0, 0)),
            pl.BlockSpec((None, D_MODEL, D_MODEL), lambda l, n: (l, 0, n)),
            pl.BlockSpec((None, 1, D_MODEL), lambda l, n: (l, 0, n)),
        ],
        out_specs=pl.BlockSpec((None, rows, D_MODEL), lambda l, n: (l, 0, n)),
        name="ada_mod",
    )(s_rows, ada_w, ada_b.reshape(DEPTH, 1, 6 * D_MODEL))


def _hidden(x_ref, mod_ref, g_ref):
    d = D_MODEL
    x = x_ref[...]
    return _mod_norm(x, g_ref[...], mod_ref[:, 0:d], mod_ref[:, d:2 * d]).astype(BF16)


def _even_in_kernel(x_ref, mod_ref, g_ref, w_ref, wuq_ref, wukk_ref, wukv_ref, qn_ref, kvn_ref,
                    ca_ref, sa_ref, cb_ref, sb_ref, q_ref, k_ref, v_ref):
    h = _hidden(x_ref, mod_ref, g_ref)
    ca, sa, cb, sb = ca_ref[...], sa_ref[...], cb_ref[...], sb_ref[...]

    def proj(c0, n):
        return jnp.dot(h, w_ref[:, c0 * LANES:(c0 + n) * LANES], preferred_element_type=F32)

    for g2 in range(2):
        p = proj(2 * g2, 2)
        for i in range(2):
            t = p[:, i * LANES:(i + 1) * LANES]
            q_ref[:, (2 * g2 + i) * LANES:(2 * g2 + i + 1) * LANES] = (
                _rope(t, ca, sa, 16) * SCALE_64).astype(BF16)
    p = proj(4, 2)
    k_ref[:, 0:LANES] = _rope(p[:, :LANES], ca, sa, 16).astype(BF16)
    v_ref[:, 0:LANES] = p[:, LANES:].astype(BF16)

    cq = _rms(proj(6, 2), qn_ref[...]).astype(BF16)
    for g2 in range(4):
        p = jnp.dot(cq, wuq_ref[:, 2 * g2 * LANES:(2 * g2 + 2) * LANES], preferred_element_type=F32)
        for i in range(2):
            t = p[:, i * LANES:(i + 1) * LANES]
            hd = 4 + 2 * g2 + i
            q_ref[:, hd * LANES:(hd + 1) * LANES] = (_rope(t, cb, sb, 8) * SCALE_B).astype(BF16)

    p = proj(8, 2)
    ckv = _rms(p[:, :LANES], kvn_ref[...]).astype(BF16)
    kpe = _rope(p[:, LANES:], cb, sb, 8)
    for g2 in range(4):
        kn = jnp.dot(ckv, wukk_ref[:, 2 * g2 * LANES:(2 * g2 + 2) * LANES], preferred_element_type=F32)
        for i in range(2):
            hd = 1 + 2 * g2 + i
            k_ref[:, hd * LANES:(hd + 1) * LANES] = (kn[:, i * LANES:(i + 1) * LANES] + kpe).astype(BF16)
    for g2 in range(2):
        vb = jnp.dot(ckv, wukv_ref[:, 2 * g2 * LANES:(2 * g2 + 2) * LANES], preferred_element_type=F32)
        v_ref[:, (1 + 2 * g2) * LANES:(3 + 2 * g2) * LANES] = vb.astype(BF16)


def _odd_in_kernel(x_ref, mod_ref, g_ref, w_ref, qkn_ref, ca_ref, sa_ref, q_ref, k_ref, v_ref):
    h = _hidden(x_ref, mod_ref, g_ref)
    ca, sa = ca_ref[...], sa_ref[...]
    gq, gk = qkn_ref[0:1, :], qkn_ref[1:2, :]

    def proj(c0):
        return jnp.dot(h, w_ref[:, c0 * LANES:(c0 + 2) * LANES], preferred_element_type=F32)

    def tiles(p):
        return p[:, :LANES], p[:, LANES:]

    for g2 in range(2):
        for i, t in enumerate(tiles(proj(2 * g2))):
            q_ref[:, (2 * g2 + i) * LANES:(2 * g2 + i + 1) * LANES] = (
                _rope(_head_norm(t, gq), ca, sa, 16) * SCALE_64).astype(BF16)
    kc, vc = tiles(proj(4))
    k_ref[:, 0:LANES] = _rope(_head_norm(kc, gk), ca, sa, 16).astype(BF16)
    v_ref[:, 0:LANES] = vc.astype(BF16)
    for g2 in range(2):
        for i, t in enumerate(tiles(proj(6 + 2 * g2))):
            j = 4 + 2 * g2 + i
            q_ref[:, j * LANES:(j + 1) * LANES] = (_rope(t, ca, sa, 16) * SCALE_64).astype(BF16)
        for i, t in enumerate(tiles(proj(10 + 2 * g2))):
            j = 1 + 2 * g2 + i
            k_ref[:, j * LANES:(j + 1) * LANES] = _rope(t, ca, sa, 16).astype(BF16)
        v_ref[:, (1 + 2 * g2) * LANES:(3 + 2 * g2) * LANES] = proj(14 + 2 * g2).astype(BF16)


def _tok_spec(width, t0=0):
    return pl.BlockSpec((None, TOK, width), lambda b, t: (b, t + t0, 0))


def _const_spec(shape, single=False):
    nd = len(shape)
    kw = dict(pipeline_mode=pl.Buffered(1)) if single else {}
    return pl.BlockSpec(shape, lambda b, t: (0,) * nd, **kw)


def _mod_spec(t0=0):
    return pl.BlockSpec((None, None, 1, 6 * D_MODEL), lambda b, t: (b, jnp.minimum(t + t0, 1), 0, 0))


def _rope_spec():
    return pl.BlockSpec((TOK, LANES), lambda b, t: (t, 0))


def _in_proj(kernel, name, x, mod, gain, weights, ropes, widths):
    bsz, ltot, _ = x.shape
    ins = [x, mod, gain] + list(weights) + list(ropes)
    in_specs = ([_tok_spec(D_MODEL), _mod_spec(), _const_spec((1, D_MODEL))]
                + [_const_spec(w.shape) for w in weights] + [_rope_spec() for _ in ropes])
    return pl.pallas_call(
        kernel,
        out_shape=[jax.ShapeDtypeStruct((bsz, ltot, w), BF16) for w in widths],
        grid=(bsz, ltot // TOK),
        in_specs=in_specs,
        out_specs=[_tok_spec(w) for w in widths],
        name=name,
    )(*ins)


def _softmax_step(carry, s, v):
    m, l, acc = carry
    m_new = jnp.maximum(m, jnp.max(s, axis=-1, keepdims=True))
    p = jnp.exp(s - m_new)
    alpha = jnp.exp(m - m_new)
    l = alpha * l + jnp.sum(p, axis=-1, keepdims=True)
    acc = alpha * acc + jnp.dot(p.astype(BF16), v, preferred_element_type=F32)
    return m_new, l, acc


def _scores(q, k):
    return lax.dot_general(q, k, (((1,), (1,)), ((), ())), preferred_element_type=F32)


def _masked(x, keep):
    return x if keep is None else jnp.where(keep, x, jnp.zeros_like(x))


def _init(m0):
    return (jnp.full((TOK, 1), m0, F32), jnp.zeros((TOK, 1), F32), jnp.zeros((TOK, LANES), F32))


def _full_unit(q, k_ref, v_ref, kt, vt, vkeep, nchunks):
    def body(c, carry):
        r0 = pl.multiple_of(c * TOK, TOK)
        k = k_ref[pl.ds(r0, TOK), kt * LANES:(kt + 1) * LANES]
        v = _masked(v_ref[pl.ds(r0, TOK), vt * LANES:(vt + 1) * LANES], vkeep)
        return _softmax_step(carry, _scores(q, k), v)
    _, l, acc = lax.fori_loop(0, nchunks, body, _init(NEG_INF))
    return acc / l


def _window_unit(q, k_ref, v_ref, vkeep, start, bias_ref, sink):
    carry = (jnp.full((TOK, 1), sink, F32), jnp.ones((TOK, 1), F32), jnp.zeros((TOK, LANES), F32))
    carry = _softmax_step(carry, _scores(q, k_ref[0:TOK, 0:LANES]), _masked(v_ref[0:TOK, 0:LANES], vkeep))
    for c in range(2):
        r0 = pl.multiple_of(start + c * TOK, LANES)
        s = _scores(q, k_ref[pl.ds(r0, TOK), 0:LANES]) + bias_ref[:, c * TOK:(c + 1) * TOK]
        carry = _softmax_step(carry, s, _masked(v_ref[pl.ds(r0, TOK), 0:LANES], vkeep))
    _, l, acc = carry
    return acc / l


def _even_attn_kernel(t0, ctx_len, ltot, sink_ref, q_ref, k_ref, v_ref, o_ref, bias_ref):
    t = pl.program_id(1) + t0
    lo = _lane((TOK, LANES)) < HEAD_DIM
    hi = jnp.logical_not(lo)
    nchunks = jnp.where(t == 0, ctx_len // TOK, ltot // TOK)

    start = jnp.clip(ctx_len - WINDOW + (t - 1) * TOK, ctx_len, ltot - 2 * TOK)
    qpos = t * TOK + lax.broadcasted_iota(jnp.int32, (TOK, 2 * TOK), 0)
    kpos = start + lax.broadcasted_iota(jnp.int32, (TOK, 2 * TOK), 1)
    reach = jnp.where(t > 0, WINDOW, -1)
    bias_ref[...] = jnp.where(jnp.abs(qpos - kpos) <= reach, 0.0, NEG_INF).astype(F32)

    for j in range(4):
        q = _tile(q_ref, j)
        o = (_window_unit(_masked(q, lo), k_ref, v_ref, lo, start, bias_ref, sink_ref[j])
             + _window_unit(_masked(q, hi), k_ref, v_ref, hi, start, bias_ref, sink_ref[4 + j]))
        o_ref[:, j * LANES:(j + 1) * LANES] = o.astype(BF16)
    for j in range(4):
        o = (_full_unit(_tile(q_ref, 4 + 2 * j), k_ref, v_ref, 1 + 2 * j, 1 + j, lo, nchunks)
             + _full_unit(_tile(q_ref, 5 + 2 * j), k_ref, v_ref, 2 + 2 * j, 1 + j, hi, nchunks))
        o_ref[:, (4 + j) * LANES:(5 + j) * LANES] = o.astype(BF16)


def _odd_attn_kernel(t0, ctx_len, ltot, lam_init, lam_ref, sub_ref, q_ref, k_ref, v_ref, o_ref):
    t = pl.program_id(1) + t0
    lo = _lane((TOK, LANES)) < HEAD_DIM
    hi = jnp.logical_not(lo)
    nchunks = jnp.where(t == 0, ctx_len // TOK, ltot // TOK)
    lp = lam_ref[...]
    lam = (jnp.exp(jnp.sum(lp[0:1] * lp[1:2], axis=-1, keepdims=True))
           - jnp.exp(jnp.sum(lp[2:3] * lp[3:4], axis=-1, keepdims=True)) + lam_init)

    for j in range(4):
        q = _tile(q_ref, j)
        o = (_full_unit(_masked(q, lo), k_ref, v_ref, 0, 0, lo, nchunks)
             + _full_unit(_masked(q, hi), k_ref, v_ref, 0, 0, hi, nchunks))
        o_ref[:, j * LANES:(j + 1) * LANES] = o.astype(BF16)
    for j in range(D_HEADS):
        q = _tile(q_ref, 4 + j)
        od = (_full_unit(_masked(q, lo), k_ref, v_ref, 1 + j, 1 + j, None, nchunks)
              - lam * _full_unit(_masked(q, hi), k_ref, v_ref, 1 + j, 1 + j, None, nchunks))
        o_ref[:, (4 + j) * LANES:(5 + j) * LANES] = (
            _rms(od, sub_ref[...]) * (1.0 - lam_init)).astype(BF16)


def _attention(kernel, name, small, q, k, v, t0, scratch=()):
    bsz, ltot, qw = q.shape
    ntile = ltot // TOK - t0
    kv_spec = lambda w: pl.BlockSpec((None, ltot, w), lambda b, t: (b, 0, 0))
    return pl.pallas_call(
        kernel,
        out_shape=jax.ShapeDtypeStruct((bsz, ntile * TOK, D_MODEL), BF16),
        grid=(bsz, ntile),
        in_specs=list(small[1]) + [_tok_spec(qw, t0), kv_spec(k.shape[2]), kv_spec(v.shape[2])],
        out_specs=_tok_spec(D_MODEL),
        scratch_shapes=list(scratch),
        name=name,
    )(*small[0], q, k, v)


def _out_proj_kernel(o_ref, w_ref, x_ref, mod_ref, g1_ref, g2_ref, x1_ref, h_ref):
    d = D_MODEL
    y = jnp.dot(o_ref[...], w_ref[...], preferred_element_type=F32)
    x1 = x_ref[...] + mod_ref[:, 2 * d:3 * d] * _rms(y, g1_ref[...])
    x1_ref[...] = x1
    h_ref[...] = _mod_norm(x1, g2_ref[...], mod_ref[:, 3 * d:4 * d], mod_ref[:, 4 * d:5 * d]).astype(BF16)


def _out_proj(o, w_out, x, mod, g1, g2, t0):
    bsz, n, _ = o.shape
    return pl.pallas_call(
        _out_proj_kernel,
        out_shape=[jax.ShapeDtypeStruct((bsz, n, D_MODEL), F32),
                   jax.ShapeDtypeStruct((bsz, n, D_MODEL), BF16)],
        grid=(bsz, n // TOK),
        in_specs=[_tok_spec(D_MODEL), _const_spec(w_out.shape), _tok_spec(D_MODEL, t0), _mod_spec(t0),
                  _const_spec((1, D_MODEL)), _const_spec((1, D_MODEL))],
        out_specs=[_tok_spec(D_MODEL), _tok_spec(D_MODEL)],
        name="out_proj",
    )(o, w_out, x, mod, g1, g2)


def _ffn_kernel(h_ref, x_ref, mod_ref, g_ref, wi_ref, wo_ref, o_ref, acc_ref):
    d = D_MODEL
    h = h_ref[...]
    for c in range(FFN_HIDDEN // FFN_CHUNK):
        c0 = c * FFN_CHUNK
        g = jnp.dot(h, wi_ref[:, c0:c0 + FFN_CHUNK], preferred_element_type=F32)
        u = jnp.dot(h, wi_ref[:, FFN_HIDDEN + c0:FFN_HIDDEN + c0 + FFN_CHUNK], preferred_element_type=F32)
        a = (g * jax.nn.sigmoid(g) * u).astype(BF16)
        part = jnp.dot(a, wo_ref[c0:c0 + FFN_CHUNK, :], preferred_element_type=F32)
        if c == 0:
            acc_ref[...] = part
        else:
            acc_ref[...] += part
    o_ref[...] = x_ref[...] + mod_ref[:, 5 * d:6 * d] * _rms(acc_ref[...], g_ref[...])


def _ffn(h, x1, mod, g3, w_in, w_out, t0):
    bsz, n, _ = h.shape
    return pl.pallas_call(
        _ffn_kernel,
        out_shape=jax.ShapeDtypeStruct((bsz, n, D_MODEL), F32),
        grid=(bsz, n // TOK),
        in_specs=[_tok_spec(D_MODEL), _tok_spec(D_MODEL), _mod_spec(t0), _const_spec((1, D_MODEL)),
                  _const_spec(w_in.shape, single=True), _const_spec(w_out.shape, single=True)],
        out_specs=_tok_spec(D_MODEL),
        scratch_shapes=[pltpu.VMEM((TOK, D_MODEL), F32)],
        name="swiglu",
    )(h, x1, mod, g3, w_in, w_out)


def _rope_tables(ctx_len, seq):
    pos = jnp.arange(seq)
    rows = (pos // GRID_W).astype(F32)[:, None]
    cols = (pos % GRID_W).astype(F32)[:, None]
    lane = jnp.arange(LANES)

    def table(active, is_col, freq_idx, first, half):
        freqs = ROPE_THETA ** (-jnp.arange(half, dtype=F32) / half)
        ang = jnp.where(is_col[None, :], cols, rows) * freqs[freq_idx][None, :]
        cos = jnp.where(active[None, :], jnp.cos(ang), 1.0)
        sin = jnp.where(active[None, :], jnp.sin(ang), 0.0) * jnp.where(first, -1.0, 1.0)[None, :]
        ident = (jnp.ones((ctx_len, LANES), F32), jnp.zeros((ctx_len, LANES), F32))
        return jnp.concatenate([ident[0], cos], 0), jnp.concatenate([ident[1], sin], 0)

    d = lane % HEAD_DIM
    ca, sa = table(lane >= 0, (d // 32) == 1, d % 16, (d % 32) < 16, 16)
    e = lane - B_NOPE
    cb, sb = table((e >= 0) & (e < B_ROPE), (e // 16) == 1, e % 8, (e % 16) < 8, 8)
    return ca, sa, cb, sb


def _pair_cols(w):
    k = w.shape[0]
    return w.reshape(k, 2, 4, HEAD_DIM).transpose(0, 2, 1, 3).reshape(k, 8 * HEAD_DIM)


def _pair_rows(w):
    n = w.shape[1]
    return w.reshape(2, 4, HEAD_DIM, n).transpose(1, 0, 2, 3).reshape(8 * HEAD_DIM, n)


def _even_weights(w_in, w_uq, w_ukv, w_out):
    k = w_in.shape[0]
    z = lambda r, n: jnp.zeros((r, n), w_in.dtype)
    w = jnp.concatenate([_pair_cols(w_in[:, :512]), w_in[:, 512:1152],
                         z(k, B_NOPE), w_in[:, 1152:1184], z(k, LANES - B_NOPE - B_ROPE)], axis=1)
    uq = w_uq.reshape(B_Q_RANK, B_HEADS, B_NOPE + B_ROPE)
    uq = jnp.pad(uq, ((0, 0), (0, 0), (0, LANES - B_NOPE - B_ROPE))).reshape(B_Q_RANK, B_HEADS * LANES)
    ukv = w_ukv.reshape(B_KV_RANK, B_HEADS, B_NOPE + B_V)
    ukk = jnp.pad(ukv[:, :, :B_NOPE], ((0, 0), (0, 0), (0, LANES - B_NOPE))).reshape(B_KV_RANK, B_HEADS * LANES)
    ukvv = ukv[:, :, B_NOPE:].reshape(B_KV_RANK, B_HEADS * B_V)
    wo = jnp.concatenate([_pair_rows(w_out[:512]), w_out[512:]], axis=0)
    return [a.astype(BF16) for a in (w, uq, ukk, ukvv, wo)]


def _odd_weights(w_in, w_out):
    w = jnp.concatenate([_pair_cols(w_in[:, :512]), w_in[:, 512:]], axis=1)
    wo = jnp.concatenate([_pair_rows(w_out[:512]), w_out[512:]], axis=0)
    return w.astype(BF16), wo.astype(BF16)


def kernel(x, c, ctx, c_ctx, ada_w, ada_b, norm_g, ffn_w_in, ffn_w_out, ev_w_in, ev_sink, ev_q_norm, ev_w_uq,
           ev_kv_norm, ev_w_ukv, ev_w_out, od_w_in, od_qk_norm, od_lambda, od_subln, od_w_out):
    bsz, seq, d = x.shape
    ctx_len = ctx.shape[1]
    ltot = ctx_len + seq
    assert d == D_MODEL and ctx_len == TOK and seq % TOK == 0 and seq % GRID_W == 0

    mod_rows = 8 * (-(-(bsz + 1) // 8))
    s_rows = jnp.zeros((mod_rows, d), F32).at[:bsz].set(c).at[bsz].set(c_ctx)
    mod_all = _ada(s_rows, ada_w, ada_b)

    ropes = _rope_tables(ctx_len, seq)
    xs = jnp.concatenate([ctx, x], axis=1)
    t0 = 0
    for l in range(DEPTH):
        last = l == DEPTH - 1
        mod = jnp.stack([jnp.broadcast_to(mod_all[l, bsz], (bsz, 6 * d)), mod_all[l, :bsz]], axis=1)
        mod = mod.reshape(bsz, 2, 1, 6 * d)
        gains = [norm_g[l, i].reshape(1, d) for i in range(4)]
        if l % 2 == 0:
            e = l // 2
            w, uq, ukk, ukvv, wo = _even_weights(ev_w_in[e], ev_w_uq[e], ev_w_ukv[e], ev_w_out[e])
            q, k, v = _in_proj(
                _even_in_kernel, "even_in_proj", xs, mod, gains[0],
                [w, uq, ukk, ukvv, ev_q_norm[e].reshape(1, -1), ev_kv_norm[e].reshape(1, -1)], ropes,
                (12 * LANES, 9 * LANES, 5 * LANES))
            t0 = 1 if last else 0
            small = ([ev_sink[e]], [pl.BlockSpec(memory_space=pltpu.SMEM)])
            o = _attention(functools.partial(_even_attn_kernel, t0, ctx_len, ltot), "even_attention",
                           small, q, k, v, t0, scratch=[pltpu.VMEM((TOK, 2 * TOK), F32)])
        else:
            i = l // 2
            w, wo = _odd_weights(od_w_in[i], od_w_out[i])
            qkn = jnp.tile(od_qk_norm[i], (1, 2))
            q, k, v = _in_proj(_odd_in_kernel, "odd_in_proj", xs, mod, gains[0], [w, qkn], ropes[:2],
                               (8 * LANES, 5 * LANES, 5 * LANES))
            t0 = 1 if last else 0
            lam_init = 0.8 - 0.6 * math.exp(-0.3 * l)
            small = ([od_lambda[i], od_subln[i].reshape(1, -1)],
                     [_const_spec((4, HEAD_DIM)), _const_spec((1, LANES))])
            o = _attention(functools.partial(_odd_attn_kernel, t0, ctx_len, ltot, lam_init), "odd_attention",
                           small, q, k, v, t0)
        x1, h2 = _out_proj(o, wo, xs, mod, gains[1], gains[2], t0)
        xs = _ffn(h2, x1, mod, gains[3], ffn_w_in[l].astype(BF16), ffn_w_out[l].astype(BF16), t0)
    return xs
```

```python
import functools
import math

import jax
import jax.numpy as jnp
from jax import lax
from jax.experimental import pallas as pl
from jax.experimental.pallas import tpu as pltpu

F32 = jnp.float32
BF16 = jnp.bfloat16

D_MODEL = 1024
DEPTH = 4
GRID_W = 64
HEAD_DIM = 64
WINDOW = 128
ROPE_THETA = 10000.0
EPS = 1e-6
NEG_INF = -1e30

A_Q_HEADS = 8
B_HEADS = 8
B_Q_RANK = 256
B_KV_RANK = 128
B_NOPE = 64
B_ROPE = 32
B_V = 64
D_HEADS = 4
FFN_HIDDEN = 2816

LANES = 128
TOK = 256
FFN_CHUNK = 256

LOG2E = math.log2(math.e)
SCALE_64 = HEAD_DIM ** -0.5 * LOG2E
SCALE_B = (B_NOPE + B_ROPE) ** -0.5 * LOG2E
UNITS = 8


def _rms(x, gain):
    ms = jnp.mean(x * x, axis=-1, keepdims=True)
    return x * lax.rsqrt(ms + EPS) * gain


def _lane(shape):
    return lax.broadcasted_iota(jnp.int32, shape, 1)


def _rope(x, cos, sin_signed, half):
    first = (_lane(x.shape) % (2 * half)) < half
    partner = jnp.where(first, pltpu.roll(x, LANES - half, 1), pltpu.roll(x, half, 1))
    return x * cos + partner * sin_signed


def _head_norm(x, gain):
    lo = _lane(x.shape) < HEAD_DIM
    x2 = x * x
    s_lo = jnp.sum(jnp.where(lo, x2, 0.0), axis=-1, keepdims=True)
    s_hi = jnp.sum(jnp.where(lo, 0.0, x2), axis=-1, keepdims=True)
    inv = jnp.where(lo, lax.rsqrt(s_lo / HEAD_DIM + EPS), lax.rsqrt(s_hi / HEAD_DIM + EPS))
    return x * inv * gain


def _tile(ref, j, n=1):
    return ref[:, j * LANES:(j + n) * LANES]


def _mod_norm(x, gain, shift, scale):
    return _rms(x, gain) * (1.0 + scale) + shift


def _ada_kernel(s_ref, w_ref, b_ref, o_ref):
    s = s_ref[...]
    a = (s * jax.nn.sigmoid(s)).astype(BF16)
    o_ref[...] = jnp.dot(a, w_ref[...].astype(BF16), preferred_element_type=F32) + b_ref[...]


def _ada(s_rows, ada_w, ada_b):
    rows = s_rows.shape[0]
    nblk = (6 * D_MODEL) // D_MODEL
    return pl.pallas_call(
        _ada_kernel,
        out_shape=jax.ShapeDtypeStruct((DEPTH, rows, 6 * D_MODEL), F32),
        grid=(DEPTH, nblk),
        in_specs=[
            pl.BlockSpec((rows, D_MODEL), lambda l, n: (0, 0)),
            pl.BlockSpec((None, D_MODEL, D_MODEL), lambda l, n: (l, 0, n)),
            pl.BlockSpec((None, 1, D_MODEL), lambda l, n: (l, 0, n)),
        ],
        out_specs=pl.BlockSpec((None, rows, D_MODEL), lambda l, n: (l, 0, n)),
        name="ada_mod",
    )(s_rows, ada_w, ada_b.reshape(DEPTH, 1, 6 * D_MODEL))


def _hidden(x_ref, mod_ref, g_ref):
    d = D_MODEL
    x = x_ref[...]
    return _mod_norm(x, g_ref[...], mod_ref[:, 0:d], mod_ref[:, d:2 * d]).astype(BF16)


def _even_in_kernel(x_ref, mod_ref, g_ref, w_ref, wuq_ref, wukk_ref, wukv_ref, qn_ref, kvn_ref,
                    ca_ref, sa_ref, cb_ref, sb_ref, q_ref, k_ref, v_ref):
    h = _hidden(x_ref, mod_ref, g_ref)
    ca, sa, cb, sb = ca_ref[...], sa_ref[...], cb_ref[...], sb_ref[...]

    def proj(c0, n):
        return jnp.dot(h, w_ref[:, c0 * LANES:(c0 + n) * LANES], preferred_element_type=F32)

    for g2 in range(2):
        p = proj(2 * g2, 2)
        for i in range(2):
            t = p[:, i * LANES:(i + 1) * LANES]
            q_ref[:, (2 * g2 + i) * LANES:(2 * g2 + i + 1) * LANES] = (
                _rope(t, ca, sa, 16) * SCALE_64).astype(BF16)
    p = proj(4, 2)
    k_ref[:, 0:LANES] = _rope(p[:, :LANES], ca, sa, 16).astype(BF16)
    v_ref[:, 0:LANES] = p[:, LANES:].astype(BF16)

    cq = _rms(proj(6, 2), qn_ref[...]).astype(BF16)
    for g2 in range(4):
        p = jnp.dot(cq, wuq_ref[:, 2 * g2 * LANES:(2 * g2 + 2) * LANES], preferred_element_type=F32)
        for i in range(2):
            t = p[:, i * LANES:(i + 1) * LANES]
            hd = 4 + 2 * g2 + i
            q_ref[:, hd * LANES:(hd + 1) * LANES] = (_rope(t, cb, sb, 8) * SCALE_B).astype(BF16)

    p = proj(8, 2)
    ckv = _rms(p[:, :LANES], kvn_ref[...]).astype(BF16)
    kpe = _rope(p[:, LANES:], cb, sb, 8)
    for g2 in range(4):
        kn = jnp.dot(ckv, wukk_ref[:, 2 * g2 * LANES:(2 * g2 + 2) * LANES], preferred_element_type=F32)
        for i in range(2):
            hd = 1 + 2 * g2 + i
            k_ref[:, hd * LANES:(hd + 1) * LANES] = (kn[:, i * LANES:(i + 1) * LANES] + kpe).astype(BF16)
    for g2 in range(2):
        vb = jnp.dot(ckv, wukv_ref[:, 2 * g2 * LANES:(2 * g2 + 2) * LANES], preferred_element_type=F32)
        v_ref[:, (1 + 2 * g2) * LANES:(3 + 2 * g2) * LANES] = vb.astype(BF16)


def _odd_in_kernel(x_ref, mod_ref, g_ref, w_ref, qkn_ref, ca_ref, sa_ref, q_ref, k_ref, v_ref):
    h = _hidden(x_ref, mod_ref, g_ref)
    ca, sa = ca_ref[...], sa_ref[...]
    gq, gk = qkn_ref[0:1, :], qkn_ref[1:2, :]

    def proj(c0):
        return jnp.dot(h, w_ref[:, c0 * LANES:(c0 + 2) * LANES], preferred_element_type=F32)

    def tiles(p):
        return p[:, :LANES], p[:, LANES:]

    for g2 in range(2):
        for i, t in enumerate(tiles(proj(2 * g2))):
            q_ref[:, (2 * g2 + i) * LANES:(2 * g2 + i + 1) * LANES] = (
                _rope(_head_norm(t, gq), ca, sa, 16) * SCALE_64).astype(BF16)
    kc, vc = tiles(proj(4))
    k_ref[:, 0:LANES] = _rope(_head_norm(kc, gk), ca, sa, 16).astype(BF16)
    v_ref[:, 0:LANES] = vc.astype(BF16)
    for g2 in range(2):
        for i, t in enumerate(tiles(proj(6 + 2 * g2))):
            j = 4 + 2 * g2 + i
            q_ref[:, j * LANES:(j + 1) * LANES] = (_rope(t, ca, sa, 16) * SCALE_64).astype(BF16)
        for i, t in enumerate(tiles(proj(10 + 2 * g2))):
            j = 1 + 2 * g2 + i
            k_ref[:, j * LANES:(j + 1) * LANES] = _rope(t, ca, sa, 16).astype(BF16)
        v_ref[:, (1 + 2 * g2) * LANES:(3 + 2 * g2) * LANES] = proj(14 + 2 * g2).astype(BF16)


def _tok_spec(width, t0=0):
    return pl.BlockSpec((None, TOK, width), lambda b, t: (b, t + t0, 0))


def _const_spec(shape, single=False):
    nd = len(shape)
    kw = dict(pipeline_mode=pl.Buffered(1)) if single else {}
    return pl.BlockSpec(shape, lambda b, t: (0,) * nd, **kw)


def _mod_spec(t0=0):
    return pl.BlockSpec((None, None, 1, 6 * D_MODEL), lambda b, t: (b, jnp.minimum(t + t0, 1), 0, 0))


def _rope_spec():
    return pl.BlockSpec((TOK, LANES), lambda b, t: (t, 0))


def _in_proj(kernel, name, x, mod, gain, weights, ropes, widths):
    bsz, ltot, _ = x.shape
    ins = [x, mod, gain] + list(weights) + list(ropes)
    in_specs = ([_tok_spec(D_MODEL), _mod_spec(), _const_spec((1, D_MODEL))]
                + [_const_spec(w.shape) for w in weights] + [_rope_spec() for _ in ropes])
    return pl.pallas_call(
        kernel,
        out_shape=[jax.ShapeDtypeStruct((bsz, ltot, w), BF16) for w in widths],
        grid=(bsz, ltot // TOK),
        in_specs=in_specs,
        out_specs=[_tok_spec(w) for w in widths],
        name=name,
    )(*ins)


def _half(shape, which):
    lane = _lane(shape)
    return lane < HEAD_DIM if which == "lo" else lane >= HEAD_DIM


def _den_lane(which):
    return HEAD_DIM if which == "lo" else 0


def _masked(x, which):
    return x if which is None else jnp.where(_half(x.shape, which), x, jnp.zeros_like(x))


def _with_ones(v, which):
    if which is None:
        return v
    ones = jnp.where(_lane(v.shape) == _den_lane(which), 1.0, 0.0).astype(v.dtype)
    return jnp.where(_half(v.shape, which), v, ones)


def _scores(q, k):
    return lax.dot_general(q, k, (((1,), (1,)), ((), ())), preferred_element_type=F32)


def _step(m, acc, l, s, v):
    s0, s1 = s[:, :LANES], s[:, LANES:]
    m_new = jnp.maximum(m, jnp.max(jnp.maximum(s0, s1), axis=-1, keepdims=True))
    p = jnp.concatenate([jnp.exp2(s0 - m_new), jnp.exp2(s1 - m_new)], axis=1)
    alpha = jnp.exp2(m - m_new)
    acc = alpha * acc + jnp.dot(p.astype(BF16), v, preferred_element_type=F32)
    if l is not None:
        l = alpha * l + jnp.sum(p, axis=-1, keepdims=True)
    return m_new, acc, l


def _normalized(acc, which, l):
    if which is not None:
        d = _den_lane(which)
        l = acc[:, d:d + 1]
    return acc / l


def _run_units(units, q_ref, k_ref, v_ref, nchunks, m_ref, acc_ref, l_ref):
    for u, unit in enumerate(units):
        m_ref[u] = jnp.full((TOK, LANES), NEG_INF, F32)
        acc_ref[u] = jnp.zeros((TOK, LANES), F32)
        if unit[4] is None:
            l_ref[u] = jnp.zeros((TOK, LANES), F32)

    def body(c, carry):
        r0 = pl.multiple_of(c * TOK, TOK)
        values = {}
        for u, (qt, qh, kt, vt, vh) in enumerate(units):
            if (vt, vh) not in values:
                values[(vt, vh)] = _with_ones(v_ref[pl.ds(r0, TOK), vt * LANES:(vt + 1) * LANES], vh)
            k = k_ref[pl.ds(r0, TOK), kt * LANES:(kt + 1) * LANES]
            s = _scores(_masked(_tile(q_ref, qt), qh), k)
            m, acc, l = _step(m_ref[u], acc_ref[u], l_ref[u] if vh is None else None, s, values[(vt, vh)])
            m_ref[u] = m
            acc_ref[u] = acc
            if vh is None:
                l_ref[u] = l
        return carry

    lax.fori_loop(0, nchunks, body, 0)
    return [_normalized(acc_ref[u], unit[4], l_ref[u] if unit[4] is None else None)
            for u, unit in enumerate(units)]


def _window_units(q_ref, k_ref, v_ref, start, bias_ref, sink_ref):
    units = [(j, h) for j in range(4) for h in ("lo", "hi")]
    state = []
    for j, h in units:
        sink = jnp.full((TOK, LANES), sink_ref[j if h == "lo" else 4 + j], F32) * LOG2E
        state.append((sink, jnp.where(_lane((TOK, LANES)) == _den_lane(h), 1.0, 0.0).astype(F32)))
    for c in range(3):
        rows = slice(0, TOK) if c == 0 else pl.ds(pl.multiple_of(start + (c - 1) * TOK, LANES), TOK)
        k = k_ref[rows, 0:LANES]
        values = {h: _with_ones(v_ref[rows, 0:LANES], h) for h in ("lo", "hi")}
        for u, (j, h) in enumerate(units):
            s = _scores(_masked(_tile(q_ref, j), h), k)
            if c > 0:
                s = s + bias_ref[:, (c - 1) * TOK:c * TOK]
            m, acc, _ = _step(state[u][0], state[u][1], None, s, values[h])
            state[u] = (m, acc)
    return [_normalized(acc, h, None) for (_, h), (_, acc) in zip(units, state)]


def _pair(o_lo, o_hi):
    return jnp.where(_half(o_lo.shape, "lo"), o_lo, o_hi)


def _even_attn_kernel(t0, ctx_len, ltot, sink_ref, q_ref, k_ref, v_ref, o_ref, bias_ref, m_ref, acc_ref, l_ref):
    t = pl.program_id(1) + t0
    nchunks = jnp.where(t == 0, ctx_len // TOK, ltot // TOK)

    start = jnp.clip(ctx_len - WINDOW + (t - 1) * TOK, ctx_len, ltot - 2 * TOK)
    qpos = t * TOK + lax.broadcasted_iota(jnp.int32, (TOK, 2 * TOK), 0)
    kpos = start + lax.broadcasted_iota(jnp.int32, (TOK, 2 * TOK), 1)
    reach = jnp.where(t > 0, WINDOW, -1)
    bias_ref[...] = jnp.where(jnp.abs(qpos - kpos) <= reach, 0.0, NEG_INF).astype(F32)

    o = _window_units(q_ref, k_ref, v_ref, start, bias_ref, sink_ref)
    for j in range(4):
        o_ref[:, j * LANES:(j + 1) * LANES] = _pair(o[2 * j], o[2 * j + 1]).astype(BF16)
    units = [(4 + h, None, 1 + h, 1 + h // 2, "lo" if h % 2 == 0 else "hi") for h in range(B_HEADS)]
    o = _run_units(units, q_ref, k_ref, v_ref, nchunks, m_ref, acc_ref, l_ref)
    for j in range(4):
        o_ref[:, (4 + j) * LANES:(5 + j) * LANES] = _pair(o[2 * j], o[2 * j + 1]).astype(BF16)


def _odd_attn_kernel(t0, ctx_len, ltot, lam_init, lam_ref, sub_ref, q_ref, k_ref, v_ref, o_ref,
                     m_ref, acc_ref, l_ref):
    t = pl.program_id(1) + t0
    nchunks = jnp.where(t == 0, ctx_len // TOK, ltot // TOK)
    lp = lam_ref[...]
    lam = (jnp.exp(jnp.sum(lp[0:1] * lp[1:2], axis=-1, keepdims=True))
           - jnp.exp(jnp.sum(lp[2:3] * lp[3:4], axis=-1, keepdims=True)) + lam_init)

    units = [(j, h, 0, 0, h) for j in range(4) for h in ("lo", "hi")]
    o = _run_units(units, q_ref, k_ref, v_ref, nchunks, m_ref, acc_ref, l_ref)
    for j in range(4):
        o_ref[:, j * LANES:(j + 1) * LANES] = _pair(o[2 * j], o[2 * j + 1]).astype(BF16)
    units = [(4 + j, h, 1 + j, 1 + j, None) for j in range(D_HEADS) for h in ("lo", "hi")]
    o = _run_units(units, q_ref, k_ref, v_ref, nchunks, m_ref, acc_ref, l_ref)
    for j in range(D_HEADS):
        od = o[2 * j] - lam * o[2 * j + 1]
        o_ref[:, (4 + j) * LANES:(5 + j) * LANES] = (
            _rms(od, sub_ref[...]) * (1.0 - lam_init)).astype(BF16)


def _attention(kernel, name, small, q, k, v, t0, scratch=()):
    bsz, ltot, qw = q.shape
    ntile = ltot // TOK - t0
    kv_spec = lambda w: pl.BlockSpec((None, ltot, w), lambda b, t: (b, 0, 0))
    return pl.pallas_call(
        kernel,
        out_shape=jax.ShapeDtypeStruct((bsz, ntile * TOK, D_MODEL), BF16),
        grid=(bsz, ntile),
        in_specs=list(small[1]) + [_tok_spec(qw, t0), kv_spec(k.shape[2]), kv_spec(v.shape[2])],
        out_specs=_tok_spec(D_MODEL),
        scratch_shapes=list(scratch) + [pltpu.VMEM((UNITS, TOK, LANES), F32)] * 3,
        name=name,
    )(*small[0], q, k, v)


def _out_proj_kernel(o_ref, w_ref, x_ref, mod_ref, g1_ref, g2_ref, x1_ref, h_ref):
    d = D_MODEL
    y = jnp.dot(o_ref[...], w_ref[...], preferred_element_type=F32)
    x1 = x_ref[...] + mod_ref[:, 2 * d:3 * d] * _rms(y, g1_ref[...])
    x1_ref[...] = x1
    h_ref[...] = _mod_norm(x1, g2_ref[...], mod_ref[:, 3 * d:4 * d], mod_ref[:, 4 * d:5 * d]).astype(BF16)


def _out_proj(o, w_out, x, mod, g1, g2, t0):
    bsz, n, _ = o.shape
    return pl.pallas_call(
        _out_proj_kernel,
        out_shape=[jax.ShapeDtypeStruct((bsz, n, D_MODEL), F32),
                   jax.ShapeDtypeStruct((bsz, n, D_MODEL), BF16)],
        grid=(bsz, n // TOK),
        in_specs=[_tok_spec(D_MODEL), _const_spec(w_out.shape), _tok_spec(D_MODEL, t0), _mod_spec(t0),
                  _const_spec((1, D_MODEL)), _const_spec((1, D_MODEL))],
        out_specs=[_tok_spec(D_MODEL), _tok_spec(D_MODEL)],
        name="out_proj",
    )(o, w_out, x, mod, g1, g2)


def _ffn_kernel(h_ref, x_ref, mod_ref, g_ref, wi_ref, wo_ref, o_ref, acc_ref):
    d = D_MODEL
    h = h_ref[...]
    for c in range(FFN_HIDDEN // FFN_CHUNK):
        c0 = c * FFN_CHUNK
        g = jnp.dot(h, wi_ref[:, c0:c0 + FFN_CHUNK], preferred_element_type=F32)
        u = jnp.dot(h, wi_ref[:, FFN_HIDDEN + c0:FFN_HIDDEN + c0 + FFN_CHUNK], preferred_element_type=F32)
        a = (g * jax.nn.sigmoid(g) * u).astype(BF16)
        part = jnp.dot(a, wo_ref[c0:c0 + FFN_CHUNK, :], preferred_element_type=F32)
        if c == 0:
            acc_ref[...] = part
        else:
            acc_ref[...] += part
    o_ref[...] = x_ref[...] + mod_ref[:, 5 * d:6 * d] * _rms(acc_ref[...], g_ref[...])


def _ffn(h, x1, mod, g3, w_in, w_out, t0):
    bsz, n, _ = h.shape
    return pl.pallas_call(
        _ffn_kernel,
        out_shape=jax.ShapeDtypeStruct((bsz, n, D_MODEL), F32),
        grid=(bsz, n // TOK),
        in_specs=[_tok_spec(D_MODEL), _tok_spec(D_MODEL), _mod_spec(t0), _const_spec((1, D_MODEL)),
                  _const_spec(w_in.shape, single=True), _const_spec(w_out.shape, single=True)],
        out_specs=_tok_spec(D_MODEL),
        scratch_shapes=[pltpu.VMEM((TOK, D_MODEL), F32)],
        name="swiglu",
    )(h, x1, mod, g3, w_in, w_out)


def _rope_tables(ctx_len, seq):
    pos = jnp.arange(seq)
    rows = (pos // GRID_W).astype(F32)[:, None]
    cols = (pos % GRID_W).astype(F32)[:, None]
    lane = jnp.arange(LANES)

    def table(active, is_col, freq_idx, first, half):
        freqs = ROPE_THETA ** (-jnp.arange(half, dtype=F32) / half)
        ang = jnp.where(is_col[None, :], cols, rows) * freqs[freq_idx][None, :]
        cos = jnp.where(active[None, :], jnp.cos(ang), 1.0)
        sin = jnp.where(active[None, :], jnp.sin(ang), 0.0) * jnp.where(first, -1.0, 1.0)[None, :]
        ident = (jnp.ones((ctx_len, LANES), F32), jnp.zeros((ctx_len, LANES), F32))
        return jnp.concatenate([ident[0], cos], 0), jnp.concatenate([ident[1], sin], 0)

    d = lane % HEAD_DIM
    ca, sa = table(lane >= 0, (d // 32) == 1, d % 16, (d % 32) < 16, 16)
    e = lane - B_NOPE
    cb, sb = table((e >= 0) & (e < B_ROPE), (e // 16) == 1, e % 8, (e % 16) < 8, 8)
    return ca, sa, cb, sb


def _pair_cols(w):
    k = w.shape[0]
    return w.reshape(k, 2, 4, HEAD_DIM).transpose(0, 2, 1, 3).reshape(k, 8 * HEAD_DIM)


def _pair_rows(w):
    n = w.shape[1]
    return w.reshape(2, 4, HEAD_DIM, n).transpose(1, 0, 2, 3).reshape(8 * HEAD_DIM, n)


def _even_weights(w_in, w_uq, w_ukv, w_out):
    k = w_in.shape[0]
    z = lambda r, n: jnp.zeros((r, n), w_in.dtype)
    w = jnp.concatenate([_pair_cols(w_in[:, :512]), w_in[:, 512:1152],
                         z(k, B_NOPE), w_in[:, 1152:1184], z(k, LANES - B_NOPE - B_ROPE)], axis=1)
    uq = w_uq.reshape(B_Q_RANK, B_HEADS, B_NOPE + B_ROPE)
    uq = jnp.pad(uq, ((0, 0), (0, 0), (0, LANES - B_NOPE - B_ROPE))).reshape(B_Q_RANK, B_HEADS * LANES)
    ukv = w_ukv.reshape(B_KV_RANK, B_HEADS, B_NOPE + B_V)
    ukk = jnp.pad(ukv[:, :, :B_NOPE], ((0, 0), (0, 0), (0, LANES - B_NOPE))).reshape(B_KV_RANK, B_HEADS * LANES)
    ukvv = ukv[:, :, B_NOPE:].reshape(B_KV_RANK, B_HEADS * B_V)
    wo = jnp.concatenate([_pair_rows(w_out[:512]), w_out[512:]], axis=0)
    return [a.astype(BF16) for a in (w, uq, ukk, ukvv, wo)]


def _odd_weights(w_in, w_out):
    w = jnp.concatenate([_pair_cols(w_in[:, :512]), w_in[:, 512:]], axis=1)
    wo = jnp.concatenate([_pair_rows(w_out[:512]), w_out[512:]], axis=0)
    return w.astype(BF16), wo.astype(BF16)


def kernel(x, c, ctx, c_ctx, ada_w, ada_b, norm_g, ffn_w_in, ffn_w_out, ev_w_in, ev_sink, ev_q_norm, ev_w_uq,
           ev_kv_norm, ev_w_ukv, ev_w_out, od_w_in, od_qk_norm, od_lambda, od_subln, od_w_out):
    bsz, seq, d = x.shape
    ctx_len = ctx.shape[1]
    ltot = ctx_len + seq
    assert d == D_MODEL and ctx_len == TOK and seq % TOK == 0 and seq % GRID_W == 0

    mod_rows = 8 * (-(-(bsz + 1) // 8))
    s_rows = jnp.zeros((mod_rows, d), F32).at[:bsz].set(c).at[bsz].set(c_ctx)
    mod_all = _ada(s_rows, ada_w, ada_b)

    ropes = _rope_tables(ctx_len, seq)
    xs = jnp.concatenate([ctx, x], axis=1)
    t0 = 0
    for l in range(DEPTH):
        last = l == DEPTH - 1
        mod = jnp.stack([jnp.broadcast_to(mod_all[l, bsz], (bsz, 6 * d)), mod_all[l, :bsz]], axis=1)
        mod = mod.reshape(bsz, 2, 1, 6 * d)
        gains = [norm_g[l, i].reshape(1, d) for i in range(4)]
        if l % 2 == 0:
            e = l // 2
            w, uq, ukk, ukvv, wo = _even_weights(ev_w_in[e], ev_w_uq[e], ev_w_ukv[e], ev_w_out[e])
            q, k, v = _in_proj(
                _even_in_kernel, "even_in_proj", xs, mod, gains[0],
                [w, uq, ukk, ukvv, ev_q_norm[e].reshape(1, -1), ev_kv_norm[e].reshape(1, -1)], ropes,
                (12 * LANES, 9 * LANES, 5 * LANES))
            t0 = 1 if last else 0
            small = ([ev_sink[e]], [pl.BlockSpec(memory_space=pltpu.SMEM)])
            o = _attention(functools.partial(_even_attn_kernel, t0, ctx_len, ltot), "even_attention",
                           small, q, k, v, t0, scratch=[pltpu.VMEM((TOK, 2 * TOK), F32)])
        else:
            i = l // 2
            w, wo = _odd_weights(od_w_in[i], od_w_out[i])
            qkn = jnp.tile(od_qk_norm[i], (1, 2))
            q, k, v = _in_proj(_odd_in_kernel, "odd_in_proj", xs, mod, gains[0], [w, qkn], ropes[:2],
                               (8 * LANES, 5 * LANES, 5 * LANES))
            t0 = 1 if last else 0
            lam_init = 0.8 - 0.6 * math.exp(-0.3 * l)
            small = ([od_lambda[i], od_subln[i].reshape(1, -1)],
                     [_const_spec((4, HEAD_DIM)), _const_spec((1, LANES))])
            o = _attention(functools.partial(_odd_attn_kernel, t0, ctx_len, ltot, lam_init), "odd_attention",
                           small, q, k, v, t0)
        x1, h2 = _out_proj(o, wo, xs, mod, gains[1], gains[2], t0)
        xs = _ffn(h2, x1, mod, gains[3], ffn_w_in[l].astype(BF16), ffn_w_out[l].astype(BF16), t0)
    return xs
```

```python
import functools
import math

import jax
import jax.numpy as jnp
from jax import lax
from jax.experimental import pallas as pl
from jax.experimental.pallas import tpu as pltpu

F32 = jnp.float32
BF16 = jnp.bfloat16

D_MODEL = 1024
DEPTH = 4
GRID_W = 64
HEAD_DIM = 64
WINDOW = 128
ROPE_THETA = 10000.0
EPS = 1e-6
NEG_INF = -1e30

A_Q_HEADS = 8
B_HEADS = 8
B_Q_RANK = 256
B_KV_RANK = 128
B_NOPE = 64
B_ROPE = 32
B_V = 64
D_HEADS = 4
FFN_HIDDEN = 2816

LANES = 128
TOK = 256
FFN_CHUNK = 256

LOG2E = math.log2(math.e)
SCALE_64 = HEAD_DIM ** -0.5 * LOG2E
SCALE_B = (B_NOPE + B_ROPE) ** -0.5 * LOG2E
UNITS = 8
CHUNKS_PER_TRIP = 9


def _rms(x, gain):
    ms = jnp.mean(x * x, axis=-1, keepdims=True)
    return x * lax.rsqrt(ms + EPS) * gain


def _lane(shape):
    return lax.broadcasted_iota(jnp.int32, shape, 1)


def _rope(x, cos, sin_signed, half):
    first = (_lane(x.shape) % (2 * half)) < half
    partner = jnp.where(first, pltpu.roll(x, LANES - half, 1), pltpu.roll(x, half, 1))
    return x * cos + partner * sin_signed


def _head_norm(x, gain):
    lo = _lane(x.shape) < HEAD_DIM
    x2 = x * x
    s_lo = jnp.sum(jnp.where(lo, x2, 0.0), axis=-1, keepdims=True)
    s_hi = jnp.sum(jnp.where(lo, 0.0, x2), axis=-1, keepdims=True)
    inv = jnp.where(lo, lax.rsqrt(s_lo / HEAD_DIM + EPS), lax.rsqrt(s_hi / HEAD_DIM + EPS))
    return x * inv * gain


def _tile(ref, j, n=1):
    return ref[:, j * LANES:(j + n) * LANES]


def _mod_norm(x, gain, shift, scale):
    return _rms(x, gain) * (1.0 + scale) + shift


def _ada_kernel(s_ref, w_ref, b_ref, o_ref):
    s = s_ref[...]
    a = (s * jax.nn.sigmoid(s)).astype(BF16)
    o_ref[...] = jnp.dot(a, w_ref[...].astype(BF16), preferred_element_type=F32) + b_ref[...]


def _ada(s_rows, ada_w, ada_b):
    rows = s_rows.shape[0]
    nblk = (6 * D_MODEL) // D_MODEL
    return pl.pallas_call(
        _ada_kernel,
        out_shape=jax.ShapeDtypeStruct((DEPTH, rows, 6 * D_MODEL), F32),
        grid=(DEPTH, nblk),
        in_specs=[
            pl.BlockSpec((rows, D_MODEL), lambda l, n: (0, 0)),
            pl.BlockSpec((None, D_MODEL, D_MODEL), lambda l, n: (l, 0, n)),
            pl.BlockSpec((None, 1, D_MODEL), lambda l, n: (l, 0, n)),
        ],
        out_specs=pl.BlockSpec((None, rows, D_MODEL), lambda l, n: (l, 0, n)),
        name="ada_mod",
    )(s_rows, ada_w, ada_b.reshape(DEPTH, 1, 6 * D_MODEL))


def _hidden(x_ref, mod_ref, g_ref):
    d = D_MODEL
    x = x_ref[...]
    return _mod_norm(x, g_ref[...], mod_ref[:, 0:d], mod_ref[:, d:2 * d]).astype(BF16)


def _even_in_kernel(x_ref, mod_ref, g_ref, w_ref, wuq_ref, wukk_ref, wukv_ref, qn_ref, kvn_ref,
                    ca_ref, sa_ref, cb_ref, sb_ref, q_ref, k_ref, v_ref):
    h = _hidden(x_ref, mod_ref, g_ref)
    ca, sa, cb, sb = ca_ref[...], sa_ref[...], cb_ref[...], sb_ref[...]

    def proj(c0, n):
        return jnp.dot(h, w_ref[:, c0 * LANES:(c0 + n) * LANES], preferred_element_type=F32)

    for g2 in range(2):
        p = proj(2 * g2, 2)
        for i in range(2):
            t = p[:, i * LANES:(i + 1) * LANES]
            q_ref[:, (2 * g2 + i) * LANES:(2 * g2 + i + 1) * LANES] = (
                _rope(t, ca, sa, 16) * SCALE_64).astype(BF16)
    p = proj(4, 2)
    k_ref[:, 0:LANES] = _rope(p[:, :LANES], ca, sa, 16).astype(BF16)
    v_ref[:, 0:LANES] = p[:, LANES:].astype(BF16)

    cq = _rms(proj(6, 2), qn_ref[...]).astype(BF16)
    for g2 in range(4):
        p = jnp.dot(cq, wuq_ref[:, 2 * g2 * LANES:(2 * g2 + 2) * LANES], preferred_element_type=F32)
        for i in range(2):
            t = p[:, i * LANES:(i + 1) * LANES]
            hd = 4 + 2 * g2 + i
            q_ref[:, hd * LANES:(hd + 1) * LANES] = (_rope(t, cb, sb, 8) * SCALE_B).astype(BF16)

    p = proj(8, 2)
    ckv = _rms(p[:, :LANES], kvn_ref[...]).astype(BF16)
    kpe = _rope(p[:, LANES:], cb, sb, 8)
    for g2 in range(4):
        kn = jnp.dot(ckv, wukk_ref[:, 2 * g2 * LANES:(2 * g2 + 2) * LANES], preferred_element_type=F32)
        for i in range(2):
            hd = 1 + 2 * g2 + i
            k_ref[:, hd * LANES:(hd + 1) * LANES] = (kn[:, i * LANES:(i + 1) * LANES] + kpe).astype(BF16)
    for g2 in range(2):
        vb = jnp.dot(ckv, wukv_ref[:, 2 * g2 * LANES:(2 * g2 + 2) * LANES], preferred_element_type=F32)
        v_ref[:, (1 + 2 * g2) * LANES:(3 + 2 * g2) * LANES] = vb.astype(BF16)


def _odd_in_kernel(x_ref, mod_ref, g_ref, w_ref, qkn_ref, ca_ref, sa_ref, q_ref, k_ref, v_ref):
    h = _hidden(x_ref, mod_ref, g_ref)
    ca, sa = ca_ref[...], sa_ref[...]
    gq, gk = qkn_ref[0:1, :], qkn_ref[1:2, :]

    def proj(c0):
        return jnp.dot(h, w_ref[:, c0 * LANES:(c0 + 2) * LANES], preferred_element_type=F32)

    def tiles(p):
        return p[:, :LANES], p[:, LANES:]

    for g2 in range(2):
        for i, t in enumerate(tiles(proj(2 * g2))):
            q_ref[:, (2 * g2 + i) * LANES:(2 * g2 + i + 1) * LANES] = (
                _rope(_head_norm(t, gq), ca, sa, 16) * SCALE_64).astype(BF16)
    kc, vc = tiles(proj(4))
    k_ref[:, 0:LANES] = _rope(_head_norm(kc, gk), ca, sa, 16).astype(BF16)
    v_ref[:, 0:LANES] = vc.astype(BF16)
    for g2 in range(2):
        for i, t in enumerate(tiles(proj(6 + 2 * g2))):
            j = 4 + 2 * g2 + i
            q_ref[:, j * LANES:(j + 1) * LANES] = (_rope(t, ca, sa, 16) * SCALE_64).astype(BF16)
        for i, t in enumerate(tiles(proj(10 + 2 * g2))):
            j = 1 + 2 * g2 + i
            k_ref[:, j * LANES:(j + 1) * LANES] = _rope(t, ca, sa, 16).astype(BF16)
        v_ref[:, (1 + 2 * g2) * LANES:(3 + 2 * g2) * LANES] = proj(14 + 2 * g2).astype(BF16)


def _tok_spec(width, t0=0):
    return pl.BlockSpec((None, TOK, width), lambda b, t: (b, t + t0, 0))


def _const_spec(shape, single=False):
    nd = len(shape)
    kw = dict(pipeline_mode=pl.Buffered(1)) if single else {}
    return pl.BlockSpec(shape, lambda b, t: (0,) * nd, **kw)


def _mod_spec(t0=0):
    return pl.BlockSpec((None, None, 1, 6 * D_MODEL), lambda b, t: (b, jnp.minimum(t + t0, 1), 0, 0))


def _rope_spec():
    return pl.BlockSpec((TOK, LANES), lambda b, t: (t, 0))


def _in_proj(kernel, name, x, mod, gain, weights, ropes, widths):
    bsz, ltot, _ = x.shape
    ins = [x, mod, gain] + list(weights) + list(ropes)
    in_specs = ([_tok_spec(D_MODEL), _mod_spec(), _const_spec((1, D_MODEL))]
                + [_const_spec(w.shape) for w in weights] + [_rope_spec() for _ in ropes])
    return pl.pallas_call(
        kernel,
        out_shape=[jax.ShapeDtypeStruct((bsz, ltot, w), BF16) for w in widths],
        grid=(bsz, ltot // TOK),
        in_specs=in_specs,
        out_specs=[_tok_spec(w) for w in widths],
        name=name,
    )(*ins)


def _half(shape, which):
    lane = _lane(shape)
    return lane < HEAD_DIM if which == "lo" else lane >= HEAD_DIM


def _den_lane(which):
    return HEAD_DIM if which == "lo" else 0


def _masked(x, which):
    return x if which is None else jnp.where(_half(x.shape, which), x, jnp.zeros_like(x))


def _with_ones(v, which):
    if which is None:
        return v
    ones = jnp.where(_lane(v.shape) == _den_lane(which), 1.0, 0.0).astype(v.dtype)
    return jnp.where(_half(v.shape, which), v, ones)


def _scores(q, k):
    return lax.dot_general(q, k, (((1,), (1,)), ((), ())), preferred_element_type=F32)


def _step(m, acc, l, s, v):
    s0, s1 = s[:, :LANES], s[:, LANES:]
    m_new = jnp.maximum(m, jnp.max(jnp.maximum(s0, s1), axis=-1, keepdims=True))
    p = jnp.concatenate([jnp.exp2(s0 - m_new), jnp.exp2(s1 - m_new)], axis=1)
    alpha = jnp.exp2(m - m_new)
    acc = alpha * acc + jnp.dot(p.astype(BF16), v, preferred_element_type=F32)
    if l is not None:
        l = alpha * l + jnp.sum(p, axis=-1, keepdims=True)
    return m_new, acc, l


def _normalized(acc, which, l):
    if which is not None:
        d = _den_lane(which)
        l = acc[:, d:d + 1]
    return acc / l


def _run_units(units, q_ref, k_ref, v_ref, nchunks, m_ref, acc_ref, l_ref):
    for u, unit in enumerate(units):
        m_ref[u] = jnp.full((TOK, LANES), NEG_INF, F32)
        acc_ref[u] = jnp.zeros((TOK, LANES), F32)
        if unit[4] is None:
            l_ref[u] = jnp.zeros((TOK, LANES), F32)

    def chunk(c):
        r0 = pl.multiple_of(c * TOK, TOK)
        values = {}
        for u, (qt, qh, kt, vt, vh) in enumerate(units):
            if (vt, vh) not in values:
                values[(vt, vh)] = _with_ones(v_ref[pl.ds(r0, TOK), vt * LANES:(vt + 1) * LANES], vh)
            k = k_ref[pl.ds(r0, TOK), kt * LANES:(kt + 1) * LANES]
            s = _scores(_masked(_tile(q_ref, qt), qh), k)
            m, acc, l = _step(m_ref[u], acc_ref[u], l_ref[u] if vh is None else None, s, values[(vt, vh)])
            m_ref[u] = m
            acc_ref[u] = acc
            if vh is None:
                l_ref[u] = l

    def trip(i, carry):
        for j in range(CHUNKS_PER_TRIP):
            chunk(i * CHUNKS_PER_TRIP + j)
        return carry

    def tail(c, carry):
        chunk(c)
        return carry

    ntrips = nchunks // CHUNKS_PER_TRIP
    lax.fori_loop(0, ntrips, trip, 0)
    lax.fori_loop(ntrips * CHUNKS_PER_TRIP, nchunks, tail, 0)
    return [_normalized(acc_ref[u], unit[4], l_ref[u] if unit[4] is None else None)
            for u, unit in enumerate(units)]


def _window_units(q_ref, k_ref, v_ref, start, bias_ref, sink_ref):
    units = [(j, h) for j in range(4) for h in ("lo", "hi")]
    state = []
    for j, h in units:
        sink = jnp.full((TOK, LANES), sink_ref[j if h == "lo" else 4 + j], F32) * LOG2E
        state.append((sink, jnp.where(_lane((TOK, LANES)) == _den_lane(h), 1.0, 0.0).astype(F32)))
    for c in range(3):
        rows = slice(0, TOK) if c == 0 else pl.ds(pl.multiple_of(start + (c - 1) * TOK, LANES), TOK)
        k = k_ref[rows, 0:LANES]
        values = {h: _with_ones(v_ref[rows, 0:LANES], h) for h in ("lo", "hi")}
        for u, (j, h) in enumerate(units):
            s = _scores(_masked(_tile(q_ref, j), h), k)
            if c > 0:
                s = s + bias_ref[:, (c - 1) * TOK:c * TOK]
            m, acc, _ = _step(state[u][0], state[u][1], None, s, values[h])
            state[u] = (m, acc)
    return [_normalized(acc, h, None) for (_, h), (_, acc) in zip(units, state)]


def _pair(o_lo, o_hi):
    return jnp.where(_half(o_lo.shape, "lo"), o_lo, o_hi)


def _even_attn_kernel(t0, ctx_len, ltot, sink_ref, q_ref, k_ref, v_ref, o_ref, bias_ref, m_ref, acc_ref, l_ref):
    t = pl.program_id(1) + t0
    nchunks = jnp.where(t == 0, ctx_len // TOK, ltot // TOK)

    start = jnp.clip(ctx_len - WINDOW + (t - 1) * TOK, ctx_len, ltot - 2 * TOK)
    qpos = t * TOK + lax.broadcasted_iota(jnp.int32, (TOK, 2 * TOK), 0)
    kpos = start + lax.broadcasted_iota(jnp.int32, (TOK, 2 * TOK), 1)
    reach = jnp.where(t > 0, WINDOW, -1)
    bias_ref[...] = jnp.where(jnp.abs(qpos - kpos) <= reach, 0.0, NEG_INF).astype(F32)

    o = _window_units(q_ref, k_ref, v_ref, start, bias_ref, sink_ref)
    for j in range(4):
        o_ref[:, j * LANES:(j + 1) * LANES] = _pair(o[2 * j], o[2 * j + 1]).astype(BF16)
    units = [(4 + h, None, 1 + h, 1 + h // 2, "lo" if h % 2 == 0 else "hi") for h in range(B_HEADS)]
    o = _run_units(units, q_ref, k_ref, v_ref, nchunks, m_ref, acc_ref, l_ref)
    for j in range(4):
        o_ref[:, (4 + j) * LANES:(5 + j) * LANES] = _pair(o[2 * j], o[2 * j + 1]).astype(BF16)


def _odd_attn_kernel(t0, ctx_len, ltot, lam_init, lam_ref, sub_ref, q_ref, k_ref, v_ref, o_ref,
                     m_ref, acc_ref, l_ref):
    t = pl.program_id(1) + t0
    nchunks = jnp.where(t == 0, ctx_len // TOK, ltot // TOK)
    lp = lam_ref[...]
    lam = (jnp.exp(jnp.sum(lp[0:1] * lp[1:2], axis=-1, keepdims=True))
           - jnp.exp(jnp.sum(lp[2:3] * lp[3:4], axis=-1, keepdims=True)) + lam_init)

    units = [(j, h, 0, 0, h) for j in range(4) for h in ("lo", "hi")]
    o = _run_units(units, q_ref, k_ref, v_ref, nchunks, m_ref, acc_ref, l_ref)
    for j in range(4):
        o_ref[:, j * LANES:(j + 1) * LANES] = _pair(o[2 * j], o[2 * j + 1]).astype(BF16)
    units = [(4 + j, h, 1 + j, 1 + j, None) for j in range(D_HEADS) for h in ("lo", "hi")]
    o = _run_units(units, q_ref, k_ref, v_ref, nchunks, m_ref, acc_ref, l_ref)
    for j in range(D_HEADS):
        od = o[2 * j] - lam * o[2 * j + 1]
        o_ref[:, (4 + j) * LANES:(5 + j) * LANES] = (
            _rms(od, sub_ref[...]) * (1.0 - lam_init)).astype(BF16)


def _attention(kernel, name, small, q, k, v, t0, scratch=()):
    bsz, ltot, qw = q.shape
    ntile = ltot // TOK - t0
    kv_spec = lambda w: pl.BlockSpec((None, ltot, w), lambda b, t: (b, 0, 0))
    return pl.pallas_call(
        kernel,
        out_shape=jax.ShapeDtypeStruct((bsz, ntile * TOK, D_MODEL), BF16),
        grid=(bsz, ntile),
        in_specs=list(small[1]) + [_tok_spec(qw, t0), kv_spec(k.shape[2]), kv_spec(v.shape[2])],
        out_specs=_tok_spec(D_MODEL),
        scratch_shapes=list(scratch) + [pltpu.VMEM((UNITS, TOK, LANES), F32)] * 3,
        name=name,
    )(*small[0], q, k, v)


def _out_proj_kernel(o_ref, w_ref, x_ref, mod_ref, g1_ref, g2_ref, x1_ref, h_ref):
    d = D_MODEL
    y = jnp.dot(o_ref[...], w_ref[...], preferred_element_type=F32)
    x1 = x_ref[...] + mod_ref[:, 2 * d:3 * d] * _rms(y, g1_ref[...])
    x1_ref[...] = x1
    h_ref[...] = _mod_norm(x1, g2_ref[...], mod_ref[:, 3 * d:4 * d], mod_ref[:, 4 * d:5 * d]).astype(BF16)


def _out_proj(o, w_out, x, mod, g1, g2, t0):
    bsz, n, _ = o.shape
    return pl.pallas_call(
        _out_proj_kernel,
        out_shape=[jax.ShapeDtypeStruct((bsz, n, D_MODEL), F32),
                   jax.ShapeDtypeStruct((bsz, n, D_MODEL), BF16)],
        grid=(bsz, n // TOK),
        in_specs=[_tok_spec(D_MODEL), _const_spec(w_out.shape), _tok_spec(D_MODEL, t0), _mod_spec(t0),
                  _const_spec((1, D_MODEL)), _const_spec((1, D_MODEL))],
        out_specs=[_tok_spec(D_MODEL), _tok_spec(D_MODEL)],
        name="out_proj",
    )(o, w_out, x, mod, g1, g2)


def _ffn_kernel(h_ref, x_ref, mod_ref, g_ref, wi_ref, wo_ref, o_ref, acc_ref):
    d = D_MODEL
    h = h_ref[...]
    for c in range(FFN_HIDDEN // FFN_CHUNK):
        c0 = c * FFN_CHUNK
        g = jnp.dot(h, wi_ref[:, c0:c0 + FFN_CHUNK], preferred_element_type=F32)
        u = jnp.dot(h, wi_ref[:, FFN_HIDDEN + c0:FFN_HIDDEN + c0 + FFN_CHUNK], preferred_element_type=F32)
        a = (g * jax.nn.sigmoid(g) * u).astype(BF16)
        part = jnp.dot(a, wo_ref[c0:c0 + FFN_CHUNK, :], preferred_element_type=F32)
        if c == 0:
            acc_ref[...] = part
        else:
            acc_ref[...] += part
    o_ref[...] = x_ref[...] + mod_ref[:, 5 * d:6 * d] * _rms(acc_ref[...], g_ref[...])


def _ffn(h, x1, mod, g3, w_in, w_out, t0):
    bsz, n, _ = h.shape
    return pl.pallas_call(
        _ffn_kernel,
        out_shape=jax.ShapeDtypeStruct((bsz, n, D_MODEL), F32),
        grid=(bsz, n // TOK),
        in_specs=[_tok_spec(D_MODEL), _tok_spec(D_MODEL), _mod_spec(t0), _const_spec((1, D_MODEL)),
                  _const_spec(w_in.shape, single=True), _const_spec(w_out.shape, single=True)],
        out_specs=_tok_spec(D_MODEL),
        scratch_shapes=[pltpu.VMEM((TOK, D_MODEL), F32)],
        name="swiglu",
    )(h, x1, mod, g3, w_in, w_out)


def _rope_tables(ctx_len, seq):
    pos = jnp.arange(seq)
    rows = (pos // GRID_W).astype(F32)[:, None]
    cols = (pos % GRID_W).astype(F32)[:, None]
    lane = jnp.arange(LANES)

    def table(active, is_col, freq_idx, first, half):
        freqs = ROPE_THETA ** (-jnp.arange(half, dtype=F32) / half)
        ang = jnp.where(is_col[None, :], cols, rows) * freqs[freq_idx][None, :]
        cos = jnp.where(active[None, :], jnp.cos(ang), 1.0)
        sin = jnp.where(active[None, :], jnp.sin(ang), 0.0) * jnp.where(first, -1.0, 1.0)[None, :]
        ident = (jnp.ones((ctx_len, LANES), F32), jnp.zeros((ctx_len, LANES), F32))
        return jnp.concatenate([ident[0], cos], 0), jnp.concatenate([ident[1], sin], 0)

    d = lane % HEAD_DIM
    ca, sa = table(lane >= 0, (d // 32) == 1, d % 16, (d % 32) < 16, 16)
    e = lane - B_NOPE
    cb, sb = table((e >= 0) & (e < B_ROPE), (e // 16) == 1, e % 8, (e % 16) < 8, 8)
    return ca, sa, cb, sb


def _pair_cols(w):
    k = w.shape[0]
    return w.reshape(k, 2, 4, HEAD_DIM).transpose(0, 2, 1, 3).reshape(k, 8 * HEAD_DIM)


def _pair_rows(w):
    n = w.shape[1]
    return w.reshape(2, 4, HEAD_DIM, n).transpose(1, 0, 2, 3).reshape(8 * HEAD_DIM, n)


def _even_weights(w_in, w_uq, w_ukv, w_out):
    k = w_in.shape[0]
    z = lambda r, n: jnp.zeros((r, n), w_in.dtype)
    w = jnp.concatenate([_pair_cols(w_in[:, :512]), w_in[:, 512:1152],
                         z(k, B_NOPE), w_in[:, 1152:1184], z(k, LANES - B_NOPE - B_ROPE)], axis=1)
    uq = w_uq.reshape(B_Q_RANK, B_HEADS, B_NOPE + B_ROPE)
    uq = jnp.pad(uq, ((0, 0), (0, 0), (0, LANES - B_NOPE - B_ROPE))).reshape(B_Q_RANK, B_HEADS * LANES)
    ukv = w_ukv.reshape(B_KV_RANK, B_HEADS, B_NOPE + B_V)
    ukk = jnp.pad(ukv[:, :, :B_NOPE], ((0, 0), (0, 0), (0, LANES - B_NOPE))).reshape(B_KV_RANK, B_HEADS * LANES)
    ukvv = ukv[:, :, B_NOPE:].reshape(B_KV_RANK, B_HEADS * B_V)
    wo = jnp.concatenate([_pair_rows(w_out[:512]), w_out[512:]], axis=0)
    return [a.astype(BF16) for a in (w, uq, ukk, ukvv, wo)]


def _odd_weights(w_in, w_out):
    w = jnp.concatenate([_pair_cols(w_in[:, :512]), w_in[:, 512:]], axis=1)
    wo = jnp.concatenate([_pair_rows(w_out[:512]), w_out[512:]], axis=0)
    return w.astype(BF16), wo.astype(BF16)


def kernel(x, c, ctx, c_ctx, ada_w, ada_b, norm_g, ffn_w_in, ffn_w_out, ev_w_in, ev_sink, ev_q_norm, ev_w_uq,
           ev_kv_norm, ev_w_ukv, ev_w_out, od_w_in, od_qk_norm, od_lambda, od_subln, od_w_out):
    bsz, seq, d = x.shape
    ctx_len = ctx.shape[1]
    ltot = ctx_len + seq
    assert d == D_MODEL and ctx_len == TOK and seq % TOK == 0 and seq % GRID_W == 0

    mod_rows = 8 * (-(-(bsz + 1) // 8))
    s_rows = jnp.zeros((mod_rows, d), F32).at[:bsz].set(c).at[bsz].set(c_ctx)
    mod_all = _ada(s_rows, ada_w, ada_b)

    ropes = _rope_tables(ctx_len, seq)
    xs = jnp.concatenate([ctx, x], axis=1)
    t0 = 0
    for l in range(DEPTH):
        last = l == DEPTH - 1
        mod = jnp.stack([jnp.broadcast_to(mod_all[l, bsz], (bsz, 6 * d)), mod_all[l, :bsz]], axis=1)
        mod = mod.reshape(bsz, 2, 1, 6 * d)
        gains = [norm_g[l, i].reshape(1, d) for i in range(4)]
        if l % 2 == 0:
            e = l // 2
            w, uq, ukk, ukvv, wo = _even_weights(ev_w_in[e], ev_w_uq[e], ev_w_ukv[e], ev_w_out[e])
            q, k, v = _in_proj(
                _even_in_kernel, "even_in_proj", xs, mod, gains[0],
                [w, uq, ukk, ukvv, ev_q_norm[e].reshape(1, -1), ev_kv_norm[e].reshape(1, -1)], ropes,
                (12 * LANES, 9 * LANES, 5 * LANES))
            t0 = 1 if last else 0
            small = ([ev_sink[e]], [pl.BlockSpec(memory_space=pltpu.SMEM)])
            o = _attention(functools.partial(_even_attn_kernel, t0, ctx_len, ltot), "even_attention",
                           small, q, k, v, t0, scratch=[pltpu.VMEM((TOK, 2 * TOK), F32)])
        else:
            i = l // 2
            w, wo = _odd_weights(od_w_in[i], od_w_out[i])
            qkn = jnp.tile(od_qk_norm[i], (1, 2))
            q, k, v = _in_proj(_odd_in_kernel, "odd_in_proj", xs, mod, gains[0], [w, qkn], ropes[:2],
                               (8 * LANES, 5 * LANES, 5 * LANES))
            t0 = 1 if last else 0
            lam_init = 0.8 - 0.6 * math.exp(-0.3 * l)
            small = ([od_lambda[i], od_subln[i].reshape(1, -1)],
                     [_const_spec((4, HEAD_DIM)), _const_spec((1, LANES))])
            o = _attention(functools.partial(_odd_attn_kernel, t0, ctx_len, ltot, lam_init), "odd_attention",
                           small, q, k, v, t0)
        x1, h2 = _out_proj(o, wo, xs, mod, gains[1], gains[2], t0)
        xs = _ffn(h2, x1, mod, gains[3], ffn_w_in[l].astype(BF16), ffn_w_out[l].astype(BF16), t0)
    return xs
```

```python
import functools
import math

import jax
import jax.numpy as jnp
from jax import lax
from jax.experimental import pallas as pl
from jax.experimental.pallas import tpu as pltpu

F32 = jnp.float32
BF16 = jnp.bfloat16

D_MODEL = 1024
DEPTH = 4
GRID_W = 64
HEAD_DIM = 64
WINDOW = 128
ROPE_THETA = 10000.0
EPS = 1e-6
NEG_INF = -1e30

A_Q_HEADS = 8
B_HEADS = 8
B_Q_RANK = 256
B_KV_RANK = 128
B_NOPE = 64
B_ROPE = 32
B_V = 64
D_HEADS = 4
FFN_HIDDEN = 2816

LANES = 128
TOK = 256
DENSE_TOK = 1024
SUB_TOK = 256
FFN_CHUNK = 256

LOG2E = math.log2(math.e)
SCALE_64 = HEAD_DIM ** -0.5 * LOG2E
SCALE_B = (B_NOPE + B_ROPE) ** -0.5 * LOG2E
UNITS = 8


def _rms(x, gain):
    ms = jnp.mean(x * x, axis=-1, keepdims=True)
    return x * lax.rsqrt(ms + EPS) * gain


def _lane(shape):
    return lax.broadcasted_iota(jnp.int32, shape, 1)


def _rope(x, cos, sin_signed, half):
    first = (_lane(x.shape) % (2 * half)) < half
    partner = jnp.where(first, pltpu.roll(x, LANES - half, 1), pltpu.roll(x, half, 1))
    return x * cos + partner * sin_signed


def _rope_swap(x, cos, sin_signed):
    return x * cos + pltpu.roll(x, LANES // 2, 1) * sin_signed


def _first_head(shape):
    return (_lane(shape) % HEAD_DIM) < HEAD_DIM // 2


def _head_norm(x, gain):
    lo = _first_head(x.shape)
    x2 = x * x
    s_lo = jnp.sum(jnp.where(lo, x2, 0.0), axis=-1, keepdims=True)
    s_hi = jnp.sum(jnp.where(lo, 0.0, x2), axis=-1, keepdims=True)
    inv = jnp.where(lo, lax.rsqrt(s_lo / HEAD_DIM + EPS), lax.rsqrt(s_hi / HEAD_DIM + EPS))
    return x * inv * gain


def _tile(ref, j, n=1):
    return ref[:, j * LANES:(j + n) * LANES]


def _ada_kernel(s_ref, w_ref, b_ref, o_ref):
    s = s_ref[...]
    a = (s * jax.nn.sigmoid(s)).astype(BF16)
    o_ref[...] = jnp.dot(a, w_ref[...].astype(BF16), preferred_element_type=F32) + b_ref[...]


def _ada(s_rows, ada_w, ada_b):
    rows = s_rows.shape[0]
    nblk = (6 * D_MODEL) // D_MODEL
    return pl.pallas_call(
        _ada_kernel,
        out_shape=jax.ShapeDtypeStruct((DEPTH, rows, 6 * D_MODEL), F32),
        grid=(DEPTH, nblk),
        in_specs=[
            pl.BlockSpec((rows, D_MODEL), lambda l, n: (0, 0)),
            pl.BlockSpec((None, D_MODEL, D_MODEL), lambda l, n: (l, 0, n)),
            pl.BlockSpec((None, 1, D_MODEL), lambda l, n: (l, 0, n)),
        ],
        out_specs=pl.BlockSpec((None, rows, D_MODEL), lambda l, n: (l, 0, n)),
        name="ada_mod",
    )(s_rows, ada_w, ada_b.reshape(DEPTH, 1, 6 * D_MODEL))


def _mod(mod_ref, i):
    return mod_ref[:, i * D_MODEL:(i + 1) * D_MODEL]


def _sub_blocks(ref):
    n = ref.shape[0]
    sub = min(SUB_TOK, n)
    return [slice(r, r + sub) for r in range(0, n, sub)]


def _hidden(x, mod_ref, g_ref):
    gs = g_ref[...] * (1.0 + _mod(mod_ref, 1))
    return (x * lax.rsqrt(jnp.mean(x * x, axis=-1, keepdims=True) + EPS) * gs + _mod(mod_ref, 0)).astype(BF16)


def _even_in_kernel(rope, x_ref, mod_ref, g_ref, w_ref, wuq_ref, wukk_ref, wukv_ref, qn_ref, kvn_ref, *rest):
    ropes, (q_ref, k_ref, v_ref) = rest[:-3], rest[-3:]
    for rows in _sub_blocks(x_ref):
        h = _hidden(x_ref[rows, :], mod_ref, g_ref)
        if rope:
            ca, sa, cb, sb = [r[rows, :] for r in ropes]
            rope_a = lambda t: _rope_swap(t, ca, sa)
            rope_b = lambda t: _rope(t, cb, sb, 8)
        else:
            rope_a = rope_b = lambda t: t

        def proj(c0, n):
            return jnp.dot(h, w_ref[:, c0 * LANES:(c0 + n) * LANES], preferred_element_type=F32)

        for g2 in range(2):
            p = proj(2 * g2, 2)
            for i in range(2):
                t = p[:, i * LANES:(i + 1) * LANES]
                q_ref[rows, (2 * g2 + i) * LANES:(2 * g2 + i + 1) * LANES] = (rope_a(t) * SCALE_64).astype(BF16)
        p = proj(4, 2)
        k_ref[rows, 0:LANES] = rope_a(p[:, :LANES]).astype(BF16)
        v_ref[rows, 0:LANES] = p[:, LANES:].astype(BF16)

        cq = _rms(proj(6, 2), qn_ref[...]).astype(BF16)
        for g2 in range(4):
            p = jnp.dot(cq, wuq_ref[:, 2 * g2 * LANES:(2 * g2 + 2) * LANES], preferred_element_type=F32)
            for i in range(2):
                t = p[:, i * LANES:(i + 1) * LANES]
                hd = 4 + 2 * g2 + i
                q_ref[rows, hd * LANES:(hd + 1) * LANES] = (rope_b(t) * SCALE_B).astype(BF16)

        p = proj(8, 2)
        ckv = _rms(p[:, :LANES], kvn_ref[...]).astype(BF16)
        kpe = rope_b(p[:, LANES:])
        for g2 in range(4):
            kn = jnp.dot(ckv, wukk_ref[:, 2 * g2 * LANES:(2 * g2 + 2) * LANES], preferred_element_type=F32)
            for i in range(2):
                hd = 1 + 2 * g2 + i
                k_ref[rows, hd * LANES:(hd + 1) * LANES] = (kn[:, i * LANES:(i + 1) * LANES] + kpe).astype(BF16)
        for g2 in range(2):
            vb = jnp.dot(ckv, wukv_ref[:, 2 * g2 * LANES:(2 * g2 + 2) * LANES], preferred_element_type=F32)
            v_ref[rows, (1 + 2 * g2) * LANES:(3 + 2 * g2) * LANES] = vb.astype(BF16)


def _odd_in_kernel(rope, x_ref, mod_ref, g_ref, w_ref, qkn_ref, *rest):
    ropes, (q_ref, k_ref, v_ref) = rest[:-3], rest[-3:]
    gq, gk = qkn_ref[0:1, :], qkn_ref[1:2, :]
    for rows in _sub_blocks(x_ref):
        h = _hidden(x_ref[rows, :], mod_ref, g_ref)
        if rope:
            ca, sa = [r[rows, :] for r in ropes]
            rope_a = lambda t: _rope_swap(t, ca, sa)
        else:
            rope_a = lambda t: t

        def proj(c0):
            return jnp.dot(h, w_ref[:, c0 * LANES:(c0 + 2) * LANES], preferred_element_type=F32)

        def tiles(p):
            return p[:, :LANES], p[:, LANES:]

        for g2 in range(2):
            for i, t in enumerate(tiles(proj(2 * g2))):
                q_ref[rows, (2 * g2 + i) * LANES:(2 * g2 + i + 1) * LANES] = (
                    rope_a(_head_norm(t, gq)) * SCALE_64).astype(BF16)
        kc, vc = tiles(proj(4))
        k_ref[rows, 0:LANES] = rope_a(_head_norm(kc, gk)).astype(BF16)
        v_ref[rows, 0:LANES] = vc.astype(BF16)
        for g2 in range(2):
            for i, t in enumerate(tiles(proj(6 + 2 * g2))):
                j = 4 + 2 * g2 + i
                q_ref[rows, j * LANES:(j + 1) * LANES] = (rope_a(t) * SCALE_64).astype(BF16)
            for i, t in enumerate(tiles(proj(10 + 2 * g2))):
                j = 1 + 2 * g2 + i
                k_ref[rows, j * LANES:(j + 1) * LANES] = rope_a(t).astype(BF16)
            v_ref[rows, (1 + 2 * g2) * LANES:(3 + 2 * g2) * LANES] = proj(14 + 2 * g2).astype(BF16)


def _rows_spec(tile, width):
    return pl.BlockSpec((None, tile, width), lambda b, t: (b, t, 0))


def _const_spec(shape, single=False):
    nd = len(shape)
    kw = dict(pipeline_mode=pl.Buffered(1)) if single else {}
    return pl.BlockSpec(shape, lambda b, t: (0,) * nd, **kw)


def _mod_spec():
    return pl.BlockSpec((None, 1, 6 * D_MODEL), lambda b, t: (b, 0, 0))


def _dense_tile(n):
    return min(DENSE_TOK, n)


def _in_proj(kernel, name, x, mod, gain, weights, ropes, widths):
    bsz, n, _ = x.shape
    tile = _dense_tile(n)
    ins = [x, mod, gain] + list(weights) + list(ropes)
    in_specs = ([_rows_spec(tile, D_MODEL), _mod_spec(), _const_spec((1, D_MODEL))]
                + [_const_spec(w.shape) for w in weights]
                + [pl.BlockSpec((tile, LANES), lambda b, t: (t, 0)) for _ in ropes])
    return pl.pallas_call(
        functools.partial(kernel, bool(ropes)),
        out_shape=[jax.ShapeDtypeStruct((bsz, n, w), BF16) for w in widths],
        grid=(bsz, n // tile),
        in_specs=in_specs,
        out_specs=[_rows_spec(tile, w) for w in widths],
        name=name,
    )(*ins)


def _half(shape, which):
    lane = _lane(shape)
    return lane < HEAD_DIM if which == "lo" else lane >= HEAD_DIM


def _query_head(q, which):
    if which is None:
        return q
    first = _first_head(q.shape)
    return jnp.where(first if which == "lo" else jnp.logical_not(first), q, jnp.zeros_like(q))


def _with_ones(v, which):
    return v if which is None else jnp.where(_half(v.shape, which), v, jnp.ones_like(v))


def _scores(q, k):
    return lax.dot_general(q, k, (((1,), (1,)), ((), ())), preferred_element_type=F32)


def _step(m, acc, l, s, v):
    s0, s1 = s[:, :LANES], s[:, LANES:]
    m_new = jnp.maximum(m, jnp.max(jnp.maximum(s0, s1), axis=-1, keepdims=True))
    p = jnp.concatenate([jnp.exp2(s0 - m_new), jnp.exp2(s1 - m_new)], axis=1)
    alpha = jnp.exp2(m - m_new)
    acc = alpha * acc + jnp.dot(p.astype(BF16), v, preferred_element_type=F32)
    if l is not None:
        l = alpha * l + jnp.sum(p, axis=-1, keepdims=True)
    return m_new, acc, l


def _pair(acc_lo, acc_hi):
    lo = _half(acc_lo.shape, "lo")
    dens = pltpu.roll(jnp.where(lo, acc_hi, acc_lo), LANES // 2, 1)
    return jnp.where(lo, acc_lo, acc_hi) / dens


def _run_units(units, q_ref, kv, swapped, self_only, guard_ref, m_ref, acc_ref, l_ref):
    k_lat, v_lat, k_ctx, v_ctx = kv
    for u, unit in enumerate(units):
        m_ref[u] = jnp.full((TOK, LANES), NEG_INF, F32)
        acc_ref[u] = jnp.zeros((TOK, LANES), F32)
        if unit[4] is None:
            l_ref[u] = jnp.zeros((TOK, LANES), F32)

    def chunk(k_ref, v_ref, c):
        rows = slice(c * TOK, (c + 1) * TOK)
        values = {}
        for u, (qt, qh, kt, vt, vh) in enumerate(units):
            if (vt, vh) not in values:
                values[(vt, vh)] = _with_ones(v_ref[rows, vt * LANES:(vt + 1) * LANES], vh)
            q = _tile(q_ref, qt)
            s = _scores(_query_head(q, qh) if swapped else q, k_ref[rows, kt * LANES:(kt + 1) * LANES])
            m, acc, l = _step(m_ref[u], acc_ref[u], l_ref[u] if vh is None else None, s, values[(vt, vh)])
            m_ref[u] = m
            acc_ref[u] = acc
            if vh is None:
                l_ref[u] = l

    @pl.when(guard_ref[0] != 0)
    def _():
        if not self_only:
            for c in range(k_lat.shape[0] // TOK):
                chunk(k_lat, v_lat, c)
        chunk(k_ctx, v_ctx, 0)

    return [acc_ref[u] if unit[4] is not None else acc_ref[u] / l_ref[u] for u, unit in enumerate(units)]


def _window_units(q_ref, kv, self_only, start, bias_ref, sink_ref):
    k_lat, v_lat, k_ctx, v_ctx = kv
    units = [(j, h) for j in range(4) for h in ("lo", "hi")]
    state = []
    for j, h in units:
        sink = jnp.full((TOK, LANES), sink_ref[j if h == "lo" else 4 + j], F32) * LOG2E
        state.append((sink, jnp.where(_half((TOK, LANES), h), 0.0, 1.0).astype(F32)))
    for c in range(1 if self_only else 3):
        if c == 0:
            k = k_ctx[:, 0:LANES]
            v = v_ctx[:, 0:LANES]
        else:
            rows = pl.ds(pl.multiple_of(start + (c - 1) * TOK, LANES), TOK)
            k = k_lat[rows, 0:LANES]
            v = v_lat[rows, 0:LANES]
        values = {h: _with_ones(v, h) for h in ("lo", "hi")}
        for u, (j, h) in enumerate(units):
            s = _scores(_query_head(_tile(q_ref, j), h), k)
            if c > 0:
                s = s + bias_ref[:, (c - 1) * TOK:c * TOK]
            m, acc, _ = _step(state[u][0], state[u][1], None, s, values[h])
            state[u] = (m, acc)
    return [acc for _, acc in state]


def _even_attn_kernel(self_only, guard_ref, sink_ref, q_ref, *rest):
    kv, (o_ref, bias_ref, m_ref, acc_ref, l_ref) = _kv_refs(self_only, rest)
    start = 0
    if not self_only:
        n_lat = kv[0].shape[0]
        t = pl.program_id(1)
        start = jnp.clip(t * TOK - WINDOW, 0, n_lat - 2 * TOK)
        qpos = t * TOK + lax.broadcasted_iota(jnp.int32, (TOK, 2 * TOK), 0)
        kpos = start + lax.broadcasted_iota(jnp.int32, (TOK, 2 * TOK), 1)
        bias_ref[...] = jnp.where(jnp.abs(qpos - kpos) <= WINDOW, 0.0, NEG_INF).astype(F32)

    o = _window_units(q_ref, kv, self_only, start, bias_ref, sink_ref)
    for j in range(4):
        o_ref[:, j * LANES:(j + 1) * LANES] = _pair(o[2 * j], o[2 * j + 1]).astype(BF16)
    units = [(4 + h, None, 1 + h, 1 + h // 2, "lo" if h % 2 == 0 else "hi") for h in range(B_HEADS)]
    o = _run_units(units, q_ref, kv, False, self_only, guard_ref, m_ref, acc_ref, l_ref)
    for j in range(4):
        o_ref[:, (4 + j) * LANES:(5 + j) * LANES] = _pair(o[2 * j], o[2 * j + 1]).astype(BF16)


def _odd_attn_kernel(lam_init, self_only, guard_ref, lam_ref, sub_ref, q_ref, *rest):
    kv, (o_ref, m_ref, acc_ref, l_ref) = _kv_refs(self_only, rest)
    lp = lam_ref[...]
    lam = (jnp.exp(jnp.sum(lp[0:1] * lp[1:2], axis=-1, keepdims=True))
           - jnp.exp(jnp.sum(lp[2:3] * lp[3:4], axis=-1, keepdims=True)) + lam_init)

    units = [(j, h, 0, 0, h) for j in range(4) for h in ("lo", "hi")]
    o = _run_units(units, q_ref, kv, True, self_only, guard_ref, m_ref, acc_ref, l_ref)
    for j in range(4):
        o_ref[:, j * LANES:(j + 1) * LANES] = _pair(o[2 * j], o[2 * j + 1]).astype(BF16)
    units = [(4 + j, h, 1 + j, 1 + j, None) for j in range(D_HEADS) for h in ("lo", "hi")]
    o = _run_units(units, q_ref, kv, True, self_only, guard_ref, m_ref, acc_ref, l_ref)
    for j in range(D_HEADS):
        od = o[2 * j] - lam * o[2 * j + 1]
        o_ref[:, (4 + j) * LANES:(5 + j) * LANES] = (
            _rms(od, sub_ref[...]) * (1.0 - lam_init)).astype(BF16)


def _kv_refs(self_only, refs):
    if self_only:
        return (None, None, refs[0], refs[1]), refs[2:]
    return tuple(refs[:4]), refs[4:]


def _attention(kernel, name, small, q, k_lat, v_lat, k_ctx, v_ctx, scratch=()):
    bsz, nq, qw = q.shape
    self_only = k_lat is None
    kv_spec = lambda a: pl.BlockSpec((None,) + a.shape[1:], lambda b, t: (b, 0, 0))
    kvs = [k_ctx, v_ctx] if self_only else [k_lat, v_lat, k_ctx, v_ctx]
    return pl.pallas_call(
        functools.partial(kernel, self_only),
        out_shape=jax.ShapeDtypeStruct((bsz, nq, D_MODEL), BF16),
        grid=(bsz, nq // TOK),
        in_specs=([pl.BlockSpec(memory_space=pltpu.SMEM)] + list(small[1]) + [_rows_spec(TOK, qw)]
                  + [kv_spec(a) for a in kvs]),
        out_specs=_rows_spec(TOK, D_MODEL),
        scratch_shapes=list(scratch) + [pltpu.VMEM((UNITS, TOK, LANES), F32)] * 3,
        name=name + ("_ctx" if self_only else ""),
    )(jnp.ones((1,), jnp.int32), *small[0], q, *kvs)


def _out_proj_kernel(o_ref, w_ref, x_ref, mod_ref, g1_ref, g2_ref, x1_ref, h_ref):
    gg = g1_ref[...] * _mod(mod_ref, 2)
    for rows in _sub_blocks(x_ref):
        y = jnp.dot(o_ref[rows, :], w_ref[...], preferred_element_type=F32)
        x1 = x_ref[rows, :] + y * lax.rsqrt(jnp.mean(y * y, axis=-1, keepdims=True) + EPS) * gg
        x1_ref[rows, :] = x1
        h_ref[rows, :] = _hidden(x1, mod_ref.at[:, 3 * D_MODEL:5 * D_MODEL], g2_ref)


def _out_proj(o, w_out, x, mod, g1, g2):
    bsz, n, _ = o.shape
    tile = _dense_tile(n)
    return pl.pallas_call(
        _out_proj_kernel,
        out_shape=[jax.ShapeDtypeStruct((bsz, n, D_MODEL), F32),
                   jax.ShapeDtypeStruct((bsz, n, D_MODEL), BF16)],
        grid=(bsz, n // tile),
        in_specs=[_rows_spec(tile, D_MODEL), _const_spec(w_out.shape), _rows_spec(tile, D_MODEL), _mod_spec(),
                  _const_spec((1, D_MODEL)), _const_spec((1, D_MODEL))],
        out_specs=[_rows_spec(tile, D_MODEL), _rows_spec(tile, D_MODEL)],
        name="out_proj",
    )(o, w_out, x, mod, g1, g2)


def _ffn_kernel(h_ref, x_ref, mod_ref, g_ref, wi_ref, wo_ref, o_ref, acc_ref):
    h = h_ref[...]
    for c in range(FFN_HIDDEN // FFN_CHUNK):
        c0 = c * FFN_CHUNK
        g = jnp.dot(h, wi_ref[:, c0:c0 + FFN_CHUNK], preferred_element_type=F32)
        u = jnp.dot(h, wi_ref[:, FFN_HIDDEN + c0:FFN_HIDDEN + c0 + FFN_CHUNK], preferred_element_type=F32)
        a = (g * jax.nn.sigmoid(g) * u).astype(BF16)
        part = jnp.dot(a, wo_ref[c0:c0 + FFN_CHUNK, :], preferred_element_type=F32)
        if c == 0:
            acc_ref[...] = part
        else:
            acc_ref[...] += part
    gg = g_ref[...] * _mod(mod_ref, 5)
    for rows in _sub_blocks(x_ref):
        f = acc_ref[rows, :]
        o_ref[rows, :] = x_ref[rows, :] + f * lax.rsqrt(jnp.mean(f * f, axis=-1, keepdims=True) + EPS) * gg


def _ffn(h, x1, mod, g3, w_in, w_out):
    bsz, n, _ = h.shape
    tile = _dense_tile(n)
    return pl.pallas_call(
        _ffn_kernel,
        out_shape=jax.ShapeDtypeStruct((bsz, n, D_MODEL), F32),
        grid=(bsz, n // tile),
        in_specs=[_rows_spec(tile, D_MODEL), _rows_spec(tile, D_MODEL), _mod_spec(), _const_spec((1, D_MODEL)),
                  _const_spec(w_in.shape, single=True), _const_spec(w_out.shape, single=True)],
        out_specs=_rows_spec(tile, D_MODEL),
        scratch_shapes=[pltpu.VMEM((tile, D_MODEL), F32)],
        name="swiglu",
    )(h, x1, mod, g3, w_in, w_out)


def _rope_tables(seq):
    pos = jnp.arange(seq)
    rows = (pos // GRID_W).astype(F32)[:, None]
    cols = (pos % GRID_W).astype(F32)[:, None]
    lane = jnp.arange(LANES)

    def table(active, is_col, freq_idx, first, half):
        freqs = ROPE_THETA ** (-jnp.arange(half, dtype=F32) / half)
        ang = jnp.where(is_col[None, :], cols, rows) * freqs[freq_idx][None, :]
        cos = jnp.where(active[None, :], jnp.cos(ang), 1.0)
        sin = jnp.where(active[None, :], jnp.sin(ang), 0.0) * jnp.where(first, -1.0, 1.0)[None, :]
        return cos, sin

    ca, sa = table(lane >= 0, (lane % 32) >= 16, lane % 16, lane < HEAD_DIM, 16)
    e = lane - B_NOPE
    cb, sb = table((e >= 0) & (e < B_ROPE), (e // 16) == 1, e % 8, (e % 16) < 8, 8)
    return ca, sa, cb, sb


def _swap_layout(w):
    k, n = w.shape
    return w.reshape(k, n // LANES, 2, 2, 2, 16).transpose(0, 1, 4, 2, 3, 5).reshape(k, n)


def _pair_cols(w):
    k = w.shape[0]
    return w.reshape(k, 2, 4, HEAD_DIM).transpose(0, 2, 1, 3).reshape(k, 8 * HEAD_DIM)


def _pair_rows(w):
    n = w.shape[1]
    return w.reshape(2, 4, HEAD_DIM, n).transpose(1, 0, 2, 3).reshape(8 * HEAD_DIM, n)


def _even_weights(w_in, w_uq, w_ukv, w_out):
    k = w_in.shape[0]
    z = lambda r, n: jnp.zeros((r, n), w_in.dtype)
    w = jnp.concatenate([_swap_layout(jnp.concatenate([_pair_cols(w_in[:, :512]), w_in[:, 512:640]], axis=1)),
                         w_in[:, 640:1152],
                         z(k, B_NOPE), w_in[:, 1152:1184], z(k, LANES - B_NOPE - B_ROPE)], axis=1)
    uq = w_uq.reshape(B_Q_RANK, B_HEADS, B_NOPE + B_ROPE)
    uq = jnp.pad(uq, ((0, 0), (0, 0), (0, LANES - B_NOPE - B_ROPE))).reshape(B_Q_RANK, B_HEADS * LANES)
    ukv = w_ukv.reshape(B_KV_RANK, B_HEADS, B_NOPE + B_V)
    ukk = jnp.pad(ukv[:, :, :B_NOPE], ((0, 0), (0, 0), (0, LANES - B_NOPE))).reshape(B_KV_RANK, B_HEADS * LANES)
    ukvv = ukv[:, :, B_NOPE:].reshape(B_KV_RANK, B_HEADS * B_V)
    wo = jnp.concatenate([_pair_rows(w_out[:512]), w_out[512:]], axis=0)
    return [a.astype(BF16) for a in (w, uq, ukk, ukvv, wo)]


def _odd_weights(w_in, w_out):
    qc_kc = _swap_layout(jnp.concatenate([_pair_cols(w_in[:, :512]), w_in[:, 512:640]], axis=1))
    w = jnp.concatenate([qc_kc, w_in[:, 640:768], _swap_layout(w_in[:, 768:1792]), w_in[:, 1792:]], axis=1)
    wo = jnp.concatenate([_pair_rows(w_out[:512]), w_out[512:]], axis=0)
    return w.astype(BF16), wo.astype(BF16)


def kernel(x, c, ctx, c_ctx, ada_w, ada_b, norm_g, ffn_w_in, ffn_w_out, ev_w_in, ev_sink, ev_q_norm, ev_w_uq,
           ev_kv_norm, ev_w_ukv, ev_w_out, od_w_in, od_qk_norm, od_lambda, od_subln, od_w_out):
    bsz, seq, d = x.shape
    ctx_len = ctx.shape[1]
    assert d == D_MODEL and ctx_len == TOK and seq % DENSE_TOK == 0 and seq % GRID_W == 0

    mod_rows = 8 * (-(-(bsz + 1) // 8))
    s_rows = jnp.zeros((mod_rows, d), F32).at[:bsz].set(c).at[bsz].set(c_ctx)
    mod_all = _ada(s_rows, ada_w, ada_b)

    ropes = _rope_tables(seq)
    xl, xc = x, ctx.reshape(1, bsz * ctx_len, d)
    per_sample = lambda a: a.reshape(bsz, ctx_len, a.shape[-1])
    for l in range(DEPTH):
        last = l == DEPTH - 1
        mod_l, mod_c = mod_all[l, :bsz].reshape(bsz, 1, 6 * d), mod_all[l, bsz].reshape(1, 1, 6 * d)
        gains = [norm_g[l, i].reshape(1, d) for i in range(4)]
        if l % 2 == 0:
            e = l // 2
            w, uq, ukk, ukvv, wo = _even_weights(ev_w_in[e], ev_w_uq[e], ev_w_ukv[e], ev_w_out[e])
            weights = [w, uq, ukk, ukvv, ev_q_norm[e].reshape(1, -1), ev_kv_norm[e].reshape(1, -1)]
            widths = (12 * LANES, 9 * LANES, 5 * LANES)
            ql, kl, vl = _in_proj(_even_in_kernel, "even_in_proj", xl, mod_l, gains[0], weights, ropes, widths)
            qc, kc, vc = [per_sample(a) for a in
                          _in_proj(_even_in_kernel, "even_in_proj_ctx", xc, mod_c, gains[0], weights, (), widths)]
            small = ([ev_sink[e]], [pl.BlockSpec(memory_space=pltpu.SMEM)])
            attn = functools.partial(_attention, _even_attn_kernel, "even_attention", small,
                                     scratch=[pltpu.VMEM((TOK, 2 * TOK), F32)])
        else:
            i = l // 2
            w, wo = _odd_weights(od_w_in[i], od_w_out[i])
            g64 = od_qk_norm[i].reshape(2, 2, 2, 16).transpose(0, 2, 1, 3)
            weights = [w, jnp.broadcast_to(g64[:, :, None], (2, 2, 2, 2, 16)).reshape(2, LANES)]
            widths = (8 * LANES, 5 * LANES, 5 * LANES)
            ql, kl, vl = _in_proj(_odd_in_kernel, "odd_in_proj", xl, mod_l, gains[0], weights, ropes[:2], widths)
            qc, kc, vc = [per_sample(a) for a in
                          _in_proj(_odd_in_kernel, "odd_in_proj_ctx", xc, mod_c, gains[0], weights, (), widths)]
            lam_init = 0.8 - 0.6 * math.exp(-0.3 * l)
            small = ([od_lambda[i], od_subln[i].reshape(1, -1)],
                     [_const_spec((4, HEAD_DIM)), _const_spec((1, LANES))])
            attn = functools.partial(_attention, functools.partial(_odd_attn_kernel, lam_init),
                                     "odd_attention", small)
        w_in, w_out = ffn_w_in[l].astype(BF16), ffn_w_out[l].astype(BF16)
        x1, h2 = _out_proj(attn(ql, kl, vl, kc, vc), wo, xl, mod_l, gains[1], gains[2])
        xl = _ffn(h2, x1, mod_l, gains[3], w_in, w_out)
        if not last:
            oc = attn(qc, None, None, kc, vc).reshape(1, bsz * ctx_len, d)
            x1, h2 = _out_proj(oc, wo, xc, mod_c, gains[1], gains[2])
            xc = _ffn(h2, x1, mod_c, gains[3], w_in, w_out)
    return xl
```

```python
import functools
import math

import jax
import jax.numpy as jnp
from jax import lax
from jax.experimental import pallas as pl
from jax.experimental.pallas import tpu as pltpu

F32 = jnp.float32
BF16 = jnp.bfloat16

D_MODEL = 1024
DEPTH = 4
GRID_W = 64
HEAD_DIM = 64
WINDOW = 128
ROPE_THETA = 10000.0
EPS = 1e-6
NEG_INF = -1e30

A_Q_HEADS = 8
B_HEADS = 8
B_Q_RANK = 256
B_KV_RANK = 128
B_NOPE = 64
B_ROPE = 32
B_V = 64
D_HEADS = 4
FFN_HIDDEN = 2816

LANES = 128
TOK = 256
DENSE_TOK = 1024
SUB_TOK = 256
FFN_CHUNK = 256

LOG2E = math.log2(math.e)
SCALE_64 = HEAD_DIM ** -0.5 * LOG2E
SCALE_B = (B_NOPE + B_ROPE) ** -0.5 * LOG2E
UNITS = 8


def _rms(x, gain):
    ms = jnp.mean(x * x, axis=-1, keepdims=True)
    return x * lax.rsqrt(ms + EPS) * gain


def _lane(shape):
    return lax.broadcasted_iota(jnp.int32, shape, 1)


def _rope(x, cos, sin_signed, half):
    first = (_lane(x.shape) % (2 * half)) < half
    partner = jnp.where(first, pltpu.roll(x, LANES - half, 1), pltpu.roll(x, half, 1))
    return x * cos + partner * sin_signed


def _rope_swap(x, cos, sin_signed):
    return x * cos + pltpu.roll(x, LANES // 2, 1) * sin_signed


def _first_head(shape):
    return (_lane(shape) % HEAD_DIM) < HEAD_DIM // 2


def _head_norm(x, gain):
    lo = _first_head(x.shape)
    x2 = x * x
    s_lo = jnp.sum(jnp.where(lo, x2, 0.0), axis=-1, keepdims=True)
    s_hi = jnp.sum(jnp.where(lo, 0.0, x2), axis=-1, keepdims=True)
    inv = jnp.where(lo, lax.rsqrt(s_lo / HEAD_DIM + EPS), lax.rsqrt(s_hi / HEAD_DIM + EPS))
    return x * inv * gain


def _tile(ref, j, n=1):
    return ref[:, j * LANES:(j + n) * LANES]


def _ada_kernel(s_ref, w_ref, b_ref, o_ref):
    s = s_ref[...]
    a = (s * jax.nn.sigmoid(s)).astype(BF16)
    o_ref[...] = jnp.dot(a, w_ref[...].astype(BF16), preferred_element_type=F32) + b_ref[...]


def _ada(s_rows, ada_w, ada_b):
    rows = s_rows.shape[0]
    nblk = (6 * D_MODEL) // D_MODEL
    return pl.pallas_call(
        _ada_kernel,
        out_shape=jax.ShapeDtypeStruct((DEPTH, rows, 6 * D_MODEL), F32),
        grid=(DEPTH, nblk),
        in_specs=[
            pl.BlockSpec((rows, D_MODEL), lambda l, n: (0, 0)),
            pl.BlockSpec((None, D_MODEL, D_MODEL), lambda l, n: (l, 0, n)),
            pl.BlockSpec((None, 1, D_MODEL), lambda l, n: (l, 0, n)),
        ],
        out_specs=pl.BlockSpec((None, rows, D_MODEL), lambda l, n: (l, 0, n)),
        name="ada_mod",
    )(s_rows, ada_w, ada_b.reshape(DEPTH, 1, 6 * D_MODEL))


def _mod(mod_ref, i):
    return mod_ref[:, i * D_MODEL:(i + 1) * D_MODEL]


def _sub_blocks(ref):
    n = ref.shape[0]
    sub = min(SUB_TOK, n)
    return [slice(r, r + sub) for r in range(0, n, sub)]


def _hidden(x, mod_ref, g_ref):
    gs = g_ref[...] * (1.0 + _mod(mod_ref, 1))
    return (x * lax.rsqrt(jnp.mean(x * x, axis=-1, keepdims=True) + EPS) * gs + _mod(mod_ref, 0)).astype(BF16)


def _even_in_kernel(rope, x_ref, mod_ref, g_ref, w_ref, wuq_ref, wukk_ref, wukv_ref, qn_ref, kvn_ref, *rest):
    ropes, (q_ref, k_ref, v_ref) = rest[:-3], rest[-3:]
    for rows in _sub_blocks(x_ref):
        h = _hidden(x_ref[rows, :], mod_ref, g_ref)
        if rope:
            ca, sa, cb, sb = [r[rows, :] for r in ropes]
            rope_a = lambda t: _rope_swap(t, ca, sa)
            rope_b = lambda t: _rope(t, cb, sb, 8)
        else:
            rope_a = rope_b = lambda t: t

        def proj(c0, n):
            return jnp.dot(h, w_ref[:, c0 * LANES:(c0 + n) * LANES], preferred_element_type=F32)

        for g2 in range(2):
            p = proj(2 * g2, 2)
            for i in range(2):
                t = p[:, i * LANES:(i + 1) * LANES]
                q_ref[rows, (2 * g2 + i) * LANES:(2 * g2 + i + 1) * LANES] = (rope_a(t) * SCALE_64).astype(BF16)
        p = proj(4, 2)
        k_ref[rows, 0:LANES] = rope_a(p[:, :LANES]).astype(BF16)
        v_ref[rows, 0:LANES] = p[:, LANES:].astype(BF16)

        cq = _rms(proj(6, 2), qn_ref[...]).astype(BF16)
        for g2 in range(4):
            p = jnp.dot(cq, wuq_ref[:, 2 * g2 * LANES:(2 * g2 + 2) * LANES], preferred_element_type=F32)
            for i in range(2):
                t = p[:, i * LANES:(i + 1) * LANES]
                hd = 4 + 2 * g2 + i
                q_ref[rows, hd * LANES:(hd + 1) * LANES] = (rope_b(t) * SCALE_B).astype(BF16)

        p = proj(8, 2)
        ckv = _rms(p[:, :LANES], kvn_ref[...]).astype(BF16)
        kpe = rope_b(p[:, LANES:])
        for g2 in range(4):
            kn = jnp.dot(ckv, wukk_ref[:, 2 * g2 * LANES:(2 * g2 + 2) * LANES], preferred_element_type=F32)
            for i in range(2):
                hd = 1 + 2 * g2 + i
                k_ref[rows, hd * LANES:(hd + 1) * LANES] = (kn[:, i * LANES:(i + 1) * LANES] + kpe).astype(BF16)
        for g2 in range(2):
            vb = jnp.dot(ckv, wukv_ref[:, 2 * g2 * LANES:(2 * g2 + 2) * LANES], preferred_element_type=F32)
            v_ref[rows, (1 + 2 * g2) * LANES:(3 + 2 * g2) * LANES] = vb.astype(BF16)


def _odd_in_kernel(rope, x_ref, mod_ref, g_ref, w_ref, qkn_ref, *rest):
    ropes, (q_ref, k_ref, v_ref) = rest[:-3], rest[-3:]
    gq, gk = qkn_ref[0:1, :], qkn_ref[1:2, :]
    for rows in _sub_blocks(x_ref):
        h = _hidden(x_ref[rows, :], mod_ref, g_ref)
        if rope:
            ca, sa = [r[rows, :] for r in ropes]
            rope_a = lambda t: _rope_swap(t, ca, sa)
        else:
            rope_a = lambda t: t

        def proj(c0):
            return jnp.dot(h, w_ref[:, c0 * LANES:(c0 + 2) * LANES], preferred_element_type=F32)

        def tiles(p):
            return p[:, :LANES], p[:, LANES:]

        for g2 in range(2):
            for i, t in enumerate(tiles(proj(2 * g2))):
                q_ref[rows, (2 * g2 + i) * LANES:(2 * g2 + i + 1) * LANES] = (
                    rope_a(_head_norm(t, gq)) * SCALE_64).astype(BF16)
        kc, vc = tiles(proj(4))
        k_ref[rows, 0:LANES] = rope_a(_head_norm(kc, gk)).astype(BF16)
        v_ref[rows, 0:LANES] = vc.astype(BF16)
        for g2 in range(2):
            for i, t in enumerate(tiles(proj(6 + 2 * g2))):
                j = 4 + 2 * g2 + i
                q_ref[rows, j * LANES:(j + 1) * LANES] = (rope_a(t) * SCALE_64).astype(BF16)
            for i, t in enumerate(tiles(proj(10 + 2 * g2))):
                j = 1 + 2 * g2 + i
                k_ref[rows, j * LANES:(j + 1) * LANES] = rope_a(t).astype(BF16)
            v_ref[rows, (1 + 2 * g2) * LANES:(3 + 2 * g2) * LANES] = proj(14 + 2 * g2).astype(BF16)


def _rows_spec(tile, width):
    return pl.BlockSpec((None, tile, width), lambda b, t: (b, t, 0))


def _const_spec(shape, single=False):
    nd = len(shape)
    kw = dict(pipeline_mode=pl.Buffered(1)) if single else {}
    return pl.BlockSpec(shape, lambda b, t: (0,) * nd, **kw)


def _mod_spec():
    return pl.BlockSpec((None, 1, 6 * D_MODEL), lambda b, t: (b, 0, 0))


def _dense_tile(n):
    return min(DENSE_TOK, n)


def _in_proj(kernel, name, x, mod, gain, weights, ropes, widths):
    bsz, n, _ = x.shape
    tile = _dense_tile(n)
    ins = [x, mod, gain] + list(weights) + list(ropes)
    in_specs = ([_rows_spec(tile, D_MODEL), _mod_spec(), _const_spec((1, D_MODEL))]
                + [_const_spec(w.shape) for w in weights]
                + [pl.BlockSpec((tile, LANES), lambda b, t: (t, 0)) for _ in ropes])
    return pl.pallas_call(
        functools.partial(kernel, bool(ropes)),
        out_shape=[jax.ShapeDtypeStruct((bsz, n, w), BF16) for w in widths],
        grid=(bsz, n // tile),
        in_specs=in_specs,
        out_specs=[_rows_spec(tile, w) for w in widths],
        name=name,
    )(*ins)


def _half(shape, which):
    lane = _lane(shape)
    return lane < HEAD_DIM if which == "lo" else lane >= HEAD_DIM


def _query_head(q, which):
    if which is None:
        return q
    first = _first_head(q.shape)
    return jnp.where(first if which == "lo" else jnp.logical_not(first), q, jnp.zeros_like(q))


def _with_ones(v, which):
    return v if which is None else jnp.where(_half(v.shape, which), v, jnp.ones_like(v))


def _scores(q, k):
    return lax.dot_general(q, k, (((1,), (1,)), ((), ())), preferred_element_type=F32)


def _pair(acc_lo, acc_hi):
    lo = _half(acc_lo.shape, "lo")
    dens = pltpu.roll(jnp.where(lo, acc_hi, acc_lo), LANES // 2, 1)
    return jnp.where(lo, acc_lo, acc_hi) / dens


def _run_units(units, q_ref, parts, swapped, guard_ref, s_ref, p_ref, acc_ref, l_ref, sinks=None):
    take = lambda ref, rows, t: ref[:, t * LANES:(t + 1) * LANES] if rows is None else ref[rows, t * LANES:(t + 1) * LANES]
    sizes = [k.shape[0] if rows is None else rows.size for k, _, rows, _ in parts]
    nchunks = sum(sizes) // TOK

    @pl.when(guard_ref[0] != 0)
    def _():
        values, tops = {}, {}

        def score(u):
            qt, qh, kt, _, _ = units[u]
            q = _tile(q_ref, qt)
            q = _query_head(q, qh) if swapped else q
            off = 0
            for (k_ref, _, rows, bias), n in zip(parts, sizes):
                s = _scores(q, take(k_ref, rows, kt))
                s_ref[u, :, off:off + n] = s if bias is None else s + bias[...]
                off += n

        def rowmax(u):
            top = None
            for c in range(nchunks):
                sc = s_ref[u, :, c * TOK:(c + 1) * TOK]
                mc = jnp.maximum(sc[:, :LANES], sc[:, LANES:])
                top = mc if top is None else jnp.maximum(top, mc)
            m = jnp.max(top, axis=-1, keepdims=True)
            tops[u] = m if sinks is None else jnp.maximum(m, sinks[u])

        def exps(u):
            m, den = tops.pop(u), None
            for c in range(nchunks):
                sc = s_ref[u, :, c * TOK:(c + 1) * TOK]
                p0, p1 = jnp.exp2(sc[:, :LANES] - m), jnp.exp2(sc[:, LANES:] - m)
                p_ref[u, :, c * TOK:(c + 1) * TOK] = jnp.concatenate([p0, p1], axis=1).astype(BF16)
                if units[u][4] is None:
                    den = p0 + p1 if den is None else den + (p0 + p1)
            if units[u][4] is None:
                l_ref[u] = jnp.broadcast_to(jnp.sum(den, axis=-1, keepdims=True), (TOK, LANES))
            elif sinks is not None:
                l_ref[u] = jnp.broadcast_to(jnp.exp2(sinks[u] - m), (TOK, LANES))

        def weigh(u):
            _, _, _, vt, vh = units[u]
            acc, off = None, 0
            for p, ((_, v_ref, rows, _), n) in enumerate(zip(parts, sizes)):
                if (p, vt, vh) not in values:
                    values[(p, vt, vh)] = _with_ones(take(v_ref, rows, vt), vh)
                part = jnp.dot(p_ref[u, :, off:off + n], values[(p, vt, vh)], preferred_element_type=F32)
                acc = part if acc is None else acc + part
                off += n
            if sinks is not None:
                acc = acc + jnp.where(_half(acc.shape, vh), 0.0, l_ref[u])
            acc_ref[u] = acc

        stages = (score, rowmax, exps, weigh)
        for step in range(len(units) + len(stages) - 1):
            for stage, fn in enumerate(stages):
                if 0 <= step - stage < len(units):
                    fn(step - stage)

    return [acc_ref[u] if unit[4] is not None else acc_ref[u] / l_ref[u] for u, unit in enumerate(units)]


def _key_parts(kv, self_only):
    k_lat, v_lat, k_ctx, v_ctx = kv
    return ([] if self_only else [(k_lat, v_lat, None, None)]) + [(k_ctx, v_ctx, None, None)]


def _even_attn_kernel(self_only, guard_ref, sink_ref, q_ref, *rest):
    kv, (o_ref, bias_ref, s_ref, p_ref, acc_ref, l_ref) = _kv_refs(self_only, rest)
    scratch = (s_ref, p_ref, acc_ref, l_ref)

    parts = [(kv[2], kv[3], None, None)]
    if not self_only:
        t = pl.program_id(1)
        start = pl.multiple_of(jnp.clip(t * TOK - WINDOW, 0, kv[0].shape[0] - 2 * TOK), LANES)
        qpos = t * TOK + lax.broadcasted_iota(jnp.int32, (TOK, 2 * TOK), 0)
        kpos = start + lax.broadcasted_iota(jnp.int32, (TOK, 2 * TOK), 1)
        bias_ref[...] = jnp.where(jnp.abs(qpos - kpos) <= WINDOW, 0.0, NEG_INF).astype(F32)
        parts.append((kv[0], kv[1], pl.ds(start, 2 * TOK), bias_ref))
    units = [(j, h, 0, 0, h) for j in range(4) for h in ("lo", "hi")]
    sinks = [jnp.full((TOK, 1), sink_ref[j if h == "lo" else 4 + j], F32) * LOG2E for j, h, _, _, _ in units]
    o = _run_units(units, q_ref, parts, True, guard_ref, *scratch, sinks=sinks)
    for j in range(4):
        o_ref[:, j * LANES:(j + 1) * LANES] = _pair(o[2 * j], o[2 * j + 1]).astype(BF16)

    units = [(4 + h, None, 1 + h, 1 + h // 2, "lo" if h % 2 == 0 else "hi") for h in range(B_HEADS)]
    o = _run_units(units, q_ref, _key_parts(kv, self_only), False, guard_ref, *scratch)
    for j in range(4):
        o_ref[:, (4 + j) * LANES:(5 + j) * LANES] = _pair(o[2 * j], o[2 * j + 1]).astype(BF16)


def _odd_attn_kernel(lam_init, self_only, guard_ref, lam_ref, sub_ref, q_ref, *rest):
    kv, (o_ref, s_ref, p_ref, acc_ref, l_ref) = _kv_refs(self_only, rest)
    lp = lam_ref[...]
    lam = (jnp.exp(jnp.sum(lp[0:1] * lp[1:2], axis=-1, keepdims=True))
           - jnp.exp(jnp.sum(lp[2:3] * lp[3:4], axis=-1, keepdims=True)) + lam_init)

    units = [(j, h, 0, 0, h) for j in range(4) for h in ("lo", "hi")]
    o = _run_units(units, q_ref, _key_parts(kv, self_only), True, guard_ref, s_ref, p_ref, acc_ref, l_ref)
    for j in range(4):
        o_ref[:, j * LANES:(j + 1) * LANES] = _pair(o[2 * j], o[2 * j + 1]).astype(BF16)
    units = [(4 + j, h, 1 + j, 1 + j, None) for j in range(D_HEADS) for h in ("lo", "hi")]
    o = _run_units(units, q_ref, _key_parts(kv, self_only), True, guard_ref, s_ref, p_ref, acc_ref, l_ref)
    for j in range(D_HEADS):
        od = o[2 * j] - lam * o[2 * j + 1]
        o_ref[:, (4 + j) * LANES:(5 + j) * LANES] = (
            _rms(od, sub_ref[...]) * (1.0 - lam_init)).astype(BF16)


def _kv_refs(self_only, refs):
    if self_only:
        return (None, None, refs[0], refs[1]), refs[2:]
    return tuple(refs[:4]), refs[4:]


def _attention(kernel, name, small, q, k_lat, v_lat, k_ctx, v_ctx, scratch=()):
    bsz, nq, qw = q.shape
    self_only = k_lat is None
    kv_spec = lambda a: pl.BlockSpec((None,) + a.shape[1:], lambda b, t: (b, 0, 0))
    kvs = [k_ctx, v_ctx] if self_only else [k_lat, v_lat, k_ctx, v_ctx]
    nk = sum(a.shape[1] for a in kvs[::2])
    return pl.pallas_call(
        functools.partial(kernel, self_only),
        out_shape=jax.ShapeDtypeStruct((bsz, nq, D_MODEL), BF16),
        grid=(bsz, nq // TOK),
        in_specs=([pl.BlockSpec(memory_space=pltpu.SMEM)] + list(small[1]) + [_rows_spec(TOK, qw)]
                  + [kv_spec(a) for a in kvs]),
        out_specs=_rows_spec(TOK, D_MODEL),
        scratch_shapes=list(scratch) + [pltpu.VMEM((UNITS, TOK, nk), F32), pltpu.VMEM((UNITS, TOK, nk), BF16),
                                        pltpu.VMEM((UNITS, TOK, LANES), F32), pltpu.VMEM((UNITS, TOK, LANES), F32)],
        name=name + ("_ctx" if self_only else ""),
    )(jnp.ones((1,), jnp.int32), *small[0], q, *kvs)


def _out_proj_kernel(o_ref, w_ref, x_ref, mod_ref, g1_ref, g2_ref, x1_ref, h_ref):
    gg = g1_ref[...] * _mod(mod_ref, 2)
    for rows in _sub_blocks(x_ref):
        y = jnp.dot(o_ref[rows, :], w_ref[...], preferred_element_type=F32)
        x1 = x_ref[rows, :] + y * lax.rsqrt(jnp.mean(y * y, axis=-1, keepdims=True) + EPS) * gg
        x1_ref[rows, :] = x1
        h_ref[rows, :] = _hidden(x1, mod_ref.at[:, 3 * D_MODEL:5 * D_MODEL], g2_ref)


def _out_proj(o, w_out, x, mod, g1, g2):
    bsz, n, _ = o.shape
    tile = _dense_tile(n)
    return pl.pallas_call(
        _out_proj_kernel,
        out_shape=[jax.ShapeDtypeStruct((bsz, n, D_MODEL), F32),
                   jax.ShapeDtypeStruct((bsz, n, D_MODEL), BF16)],
        grid=(bsz, n // tile),
        in_specs=[_rows_spec(tile, D_MODEL), _const_spec(w_out.shape), _rows_spec(tile, D_MODEL), _mod_spec(),
                  _const_spec((1, D_MODEL)), _const_spec((1, D_MODEL))],
        out_specs=[_rows_spec(tile, D_MODEL), _rows_spec(tile, D_MODEL)],
        name="out_proj",
    )(o, w_out, x, mod, g1, g2)


def _ffn_kernel(h_ref, x_ref, mod_ref, g_ref, wi_ref, wo_ref, o_ref, acc_ref):
    h = h_ref[...]
    for c in range(FFN_HIDDEN // FFN_CHUNK):
        c0 = c * FFN_CHUNK
        g = jnp.dot(h, wi_ref[:, c0:c0 + FFN_CHUNK], preferred_element_type=F32)
        u = jnp.dot(h, wi_ref[:, FFN_HIDDEN + c0:FFN_HIDDEN + c0 + FFN_CHUNK], preferred_element_type=F32)
        a = (g * jax.nn.sigmoid(g) * u).astype(BF16)
        part = jnp.dot(a, wo_ref[c0:c0 + FFN_CHUNK, :], preferred_element_type=F32)
        if c == 0:
            acc_ref[...] = part
        else:
            acc_ref[...] += part
    gg = g_ref[...] * _mod(mod_ref, 5)
    for rows in _sub_blocks(x_ref):
        f = acc_ref[rows, :]
        o_ref[rows, :] = x_ref[rows, :] + f * lax.rsqrt(jnp.mean(f * f, axis=-1, keepdims=True) + EPS) * gg


def _ffn(h, x1, mod, g3, w_in, w_out):
    bsz, n, _ = h.shape
    tile = _dense_tile(n)
    return pl.pallas_call(
        _ffn_kernel,
        out_shape=jax.ShapeDtypeStruct((bsz, n, D_MODEL), F32),
        grid=(bsz, n // tile),
        in_specs=[_rows_spec(tile, D_MODEL), _rows_spec(tile, D_MODEL), _mod_spec(), _const_spec((1, D_MODEL)),
                  _const_spec(w_in.shape, single=True), _const_spec(w_out.shape, single=True)],
        out_specs=_rows_spec(tile, D_MODEL),
        scratch_shapes=[pltpu.VMEM((tile, D_MODEL), F32)],
        name="swiglu",
    )(h, x1, mod, g3, w_in, w_out)


def _rope_tables(seq):
    pos = jnp.arange(seq)
    rows = (pos // GRID_W).astype(F32)[:, None]
    cols = (pos % GRID_W).astype(F32)[:, None]
    lane = jnp.arange(LANES)

    def table(active, is_col, freq_idx, first, half):
        freqs = ROPE_THETA ** (-jnp.arange(half, dtype=F32) / half)
        ang = jnp.where(is_col[None, :], cols, rows) * freqs[freq_idx][None, :]
        cos = jnp.where(active[None, :], jnp.cos(ang), 1.0)
        sin = jnp.where(active[None, :], jnp.sin(ang), 0.0) * jnp.where(first, -1.0, 1.0)[None, :]
        return cos, sin

    ca, sa = table(lane >= 0, (lane % 32) >= 16, lane % 16, lane < HEAD_DIM, 16)
    e = lane - B_NOPE
    cb, sb = table((e >= 0) & (e < B_ROPE), (e // 16) == 1, e % 8, (e % 16) < 8, 8)
    return ca, sa, cb, sb


def _swap_layout(w):
    k, n = w.shape
    return w.reshape(k, n // LANES, 2, 2, 2, 16).transpose(0, 1, 4, 2, 3, 5).reshape(k, n)


def _pair_cols(w):
    k = w.shape[0]
    return w.reshape(k, 2, 4, HEAD_DIM).transpose(0, 2, 1, 3).reshape(k, 8 * HEAD_DIM)


def _pair_rows(w):
    n = w.shape[1]
    return w.reshape(2, 4, HEAD_DIM, n).transpose(1, 0, 2, 3).reshape(8 * HEAD_DIM, n)


def _even_weights(w_in, w_uq, w_ukv, w_out):
    k = w_in.shape[0]
    z = lambda r, n: jnp.zeros((r, n), w_in.dtype)
    w = jnp.concatenate([_swap_layout(jnp.concatenate([_pair_cols(w_in[:, :512]), w_in[:, 512:640]], axis=1)),
                         w_in[:, 640:1152],
                         z(k, B_NOPE), w_in[:, 1152:1184], z(k, LANES - B_NOPE - B_ROPE)], axis=1)
    uq = w_uq.reshape(B_Q_RANK, B_HEADS, B_NOPE + B_ROPE)
    uq = jnp.pad(uq, ((0, 0), (0, 0), (0, LANES - B_NOPE - B_ROPE))).reshape(B_Q_RANK, B_HEADS * LANES)
    ukv = w_ukv.reshape(B_KV_RANK, B_HEADS, B_NOPE + B_V)
    ukk = jnp.pad(ukv[:, :, :B_NOPE], ((0, 0), (0, 0), (0, LANES - B_NOPE))).reshape(B_KV_RANK, B_HEADS * LANES)
    ukvv = ukv[:, :, B_NOPE:].reshape(B_KV_RANK, B_HEADS * B_V)
    wo = jnp.concatenate([_pair_rows(w_out[:512]), w_out[512:]], axis=0)
    return [a.astype(BF16) for a in (w, uq, ukk, ukvv, wo)]


def _odd_weights(w_in, w_out):
    qc_kc = _swap_layout(jnp.concatenate([_pair_cols(w_in[:, :512]), w_in[:, 512:640]], axis=1))
    w = jnp.concatenate([qc_kc, w_in[:, 640:768], _swap_layout(w_in[:, 768:1792]), w_in[:, 1792:]], axis=1)
    wo = jnp.concatenate([_pair_rows(w_out[:512]), w_out[512:]], axis=0)
    return w.astype(BF16), wo.astype(BF16)


def kernel(x, c, ctx, c_ctx, ada_w, ada_b, norm_g, ffn_w_in, ffn_w_out, ev_w_in, ev_sink, ev_q_norm, ev_w_uq,
           ev_kv_norm, ev_w_ukv, ev_w_out, od_w_in, od_qk_norm, od_lambda, od_subln, od_w_out):
    bsz, seq, d = x.shape
    ctx_len = ctx.shape[1]
    assert d == D_MODEL and ctx_len == TOK and seq % DENSE_TOK == 0 and seq % GRID_W == 0

    mod_rows = 8 * (-(-(bsz + 1) // 8))
    s_rows = jnp.zeros((mod_rows, d), F32).at[:bsz].set(c).at[bsz].set(c_ctx)
    mod_all = _ada(s_rows, ada_w, ada_b)

    ropes = _rope_tables(seq)
    xl, xc = x, ctx.reshape(1, bsz * ctx_len, d)
    per_sample = lambda a: a.reshape(bsz, ctx_len, a.shape[-1])
    for l in range(DEPTH):
        last = l == DEPTH - 1
        mod_l, mod_c = mod_all[l, :bsz].reshape(bsz, 1, 6 * d), mod_all[l, bsz].reshape(1, 1, 6 * d)
        gains = [norm_g[l, i].reshape(1, d) for i in range(4)]
        if l % 2 == 0:
            e = l // 2
            w, uq, ukk, ukvv, wo = _even_weights(ev_w_in[e], ev_w_uq[e], ev_w_ukv[e], ev_w_out[e])
            weights = [w, uq, ukk, ukvv, ev_q_norm[e].reshape(1, -1), ev_kv_norm[e].reshape(1, -1)]
            widths = (12 * LANES, 9 * LANES, 5 * LANES)
            ql, kl, vl = _in_proj(_even_in_kernel, "even_in_proj", xl, mod_l, gains[0], weights, ropes, widths)
            qc, kc, vc = [per_sample(a) for a in
                          _in_proj(_even_in_kernel, "even_in_proj_ctx", xc, mod_c, gains[0], weights, (), widths)]
            small = ([ev_sink[e]], [pl.BlockSpec(memory_space=pltpu.SMEM)])
            attn = functools.partial(_attention, _even_attn_kernel, "even_attention", small,
                                     scratch=[pltpu.VMEM((TOK, 2 * TOK), F32)])
        else:
            i = l // 2
            w, wo = _odd_weights(od_w_in[i], od_w_out[i])
            g64 = od_qk_norm[i].reshape(2, 2, 2, 16).transpose(0, 2, 1, 3)
            weights = [w, jnp.broadcast_to(g64[:, :, None], (2, 2, 2, 2, 16)).reshape(2, LANES)]
            widths = (8 * LANES, 5 * LANES, 5 * LANES)
            ql, kl, vl = _in_proj(_odd_in_kernel, "odd_in_proj", xl, mod_l, gains[0], weights, ropes[:2], widths)
            qc, kc, vc = [per_sample(a) for a in
                          _in_proj(_odd_in_kernel, "odd_in_proj_ctx", xc, mod_c, gains[0], weights, (), widths)]
            lam_init = 0.8 - 0.6 * math.exp(-0.3 * l)
            small = ([od_lambda[i], od_subln[i].reshape(1, -1)],
                     [_const_spec((4, HEAD_DIM)), _const_spec((1, LANES))])
            attn = functools.partial(_attention, functools.partial(_odd_attn_kernel, lam_init),
                                     "odd_attention", small)
        w_in, w_out = ffn_w_in[l].astype(BF16), ffn_w_out[l].astype(BF16)
        x1, h2 = _out_proj(attn(ql, kl, vl, kc, vc), wo, xl, mod_l, gains[1], gains[2])
        xl = _ffn(h2, x1, mod_l, gains[3], w_in, w_out)
        if not last:
            oc = attn(qc, None, None, kc, vc).reshape(1, bsz * ctx_len, d)
            x1, h2 = _out_proj(oc, wo, xc, mod_c, gains[1], gains[2])
            xc = _ffn(h2, x1, mod_c, gains[3], w_in, w_out)
    return xl
```

```python
import functools
import math

import jax
import jax.numpy as jnp
from jax import lax
from jax.experimental import pallas as pl
from jax.experimental.pallas import tpu as pltpu

F32 = jnp.float32
BF16 = jnp.bfloat16

D_MODEL = 1024
DEPTH = 4
GRID_W = 64
HEAD_DIM = 64
WINDOW = 128
ROPE_THETA = 10000.0
EPS = 1e-6
NEG_INF = -1e30

A_Q_HEADS = 8
B_HEADS = 8
B_Q_RANK = 256
B_KV_RANK = 128
B_NOPE = 64
B_ROPE = 32
B_V = 64
D_HEADS = 4
FFN_HIDDEN = 2816

LANES = 128
TOK = 256
DENSE_TOK = 1024
SUB_TOK = 256
FFN_CHUNK = 256

LOG2E = math.log2(math.e)
SCALE_64 = HEAD_DIM ** -0.5 * LOG2E
SCALE_B = (B_NOPE + B_ROPE) ** -0.5 * LOG2E
UNITS = 16
RING = 8


def _rms(x, gain):
    ms = jnp.mean(x * x, axis=-1, keepdims=True)
    return x * lax.rsqrt(ms + EPS) * gain


def _lane(shape):
    return lax.broadcasted_iota(jnp.int32, shape, 1)


def _rope(x, cos, sin_signed, half):
    first = (_lane(x.shape) % (2 * half)) < half
    partner = jnp.where(first, pltpu.roll(x, LANES - half, 1), pltpu.roll(x, half, 1))
    return x * cos + partner * sin_signed


def _rope_swap(x, cos, sin_signed):
    return x * cos + pltpu.roll(x, LANES // 2, 1) * sin_signed


def _first_head(shape):
    return (_lane(shape) % HEAD_DIM) < HEAD_DIM // 2


def _head_norm(x, gain):
    lo = _first_head(x.shape)
    x2 = x * x
    s_lo = jnp.sum(jnp.where(lo, x2, 0.0), axis=-1, keepdims=True)
    s_hi = jnp.sum(jnp.where(lo, 0.0, x2), axis=-1, keepdims=True)
    inv = jnp.where(lo, lax.rsqrt(s_lo / HEAD_DIM + EPS), lax.rsqrt(s_hi / HEAD_DIM + EPS))
    return x * inv * gain


def _tile(ref, j, n=1):
    return ref[:, j * LANES:(j + n) * LANES]


def _ada_kernel(s_ref, w_ref, b_ref, o_ref):
    s = s_ref[...]
    a = (s * jax.nn.sigmoid(s)).astype(BF16)
    o_ref[...] = jnp.dot(a, w_ref[...].astype(BF16), preferred_element_type=F32) + b_ref[...]


def _ada(s_rows, ada_w, ada_b):
    rows = s_rows.shape[0]
    nblk = (6 * D_MODEL) // D_MODEL
    return pl.pallas_call(
        _ada_kernel,
        out_shape=jax.ShapeDtypeStruct((DEPTH, rows, 6 * D_MODEL), F32),
        grid=(DEPTH, nblk),
        in_specs=[
            pl.BlockSpec((rows, D_MODEL), lambda l, n: (0, 0)),
            pl.BlockSpec((None, D_MODEL, D_MODEL), lambda l, n: (l, 0, n)),
            pl.BlockSpec((None, 1, D_MODEL), lambda l, n: (l, 0, n)),
        ],
        out_specs=pl.BlockSpec((None, rows, D_MODEL), lambda l, n: (l, 0, n)),
        name="ada_mod",
    )(s_rows, ada_w, ada_b.reshape(DEPTH, 1, 6 * D_MODEL))


def _mod(mod_ref, i):
    return mod_ref[:, i * D_MODEL:(i + 1) * D_MODEL]


def _sub_blocks(ref):
    n = ref.shape[0]
    sub = min(SUB_TOK, n)
    return [slice(r, r + sub) for r in range(0, n, sub)]


def _pipeline(stages, n):
    for step in range(n + len(stages) - 1):
        for s, fn in enumerate(stages):
            if 0 <= step - s < n:
                fn(step - s)


def _hidden(x, mod_ref, g_ref):
    gs = g_ref[...] * (1.0 + _mod(mod_ref, 1))
    return (x * lax.rsqrt(jnp.mean(x * x, axis=-1, keepdims=True) + EPS) * gs + _mod(mod_ref, 0)).astype(BF16)


def _even_in_kernel(rope, x_ref, mod_ref, g_ref, w_ref, wuq_ref, wukk_ref, wukv_ref, qn_ref, kvn_ref, *rest):
    ropes, (q_ref, k_ref, v_ref) = rest[:-3], rest[-3:]
    blocks = _sub_blocks(x_ref)
    held = [{} for _ in blocks]

    def rotary(i):
        if not rope:
            return (lambda t: t), (lambda t: t)
        ca, sa, cb, sb = [r[blocks[i], :] for r in ropes]
        return (lambda t: _rope_swap(t, ca, sa)), (lambda t: _rope(t, cb, sb, 8))

    def normalise(i):
        held[i]["h"] = _hidden(x_ref[blocks[i], :], mod_ref, g_ref)

    def project(i):
        rows, h, (rope_a, rope_b) = blocks[i], held[i].pop("h"), rotary(i)

        def proj(c0, n):
            return jnp.dot(h, w_ref[:, c0 * LANES:(c0 + n) * LANES], preferred_element_type=F32)

        for g2 in range(2):
            p = proj(2 * g2, 2)
            for j in range(2):
                t = p[:, j * LANES:(j + 1) * LANES]
                q_ref[rows, (2 * g2 + j) * LANES:(2 * g2 + j + 1) * LANES] = (rope_a(t) * SCALE_64).astype(BF16)
        p = proj(4, 2)
        k_ref[rows, 0:LANES] = rope_a(p[:, :LANES]).astype(BF16)
        v_ref[rows, 0:LANES] = p[:, LANES:].astype(BF16)
        held[i]["cq"] = _rms(proj(6, 2), qn_ref[...]).astype(BF16)
        p = proj(8, 2)
        held[i]["ckv"] = _rms(p[:, :LANES], kvn_ref[...]).astype(BF16)
        held[i]["kpe"] = rope_b(p[:, LANES:])

    def expand(i):
        rows, (_, rope_b) = blocks[i], rotary(i)
        cq, ckv, kpe = held[i].pop("cq"), held[i].pop("ckv"), held[i].pop("kpe")
        for g2 in range(4):
            p = jnp.dot(cq, wuq_ref[:, 2 * g2 * LANES:(2 * g2 + 2) * LANES], preferred_element_type=F32)
            for j in range(2):
                t = p[:, j * LANES:(j + 1) * LANES]
                hd = 4 + 2 * g2 + j
                q_ref[rows, hd * LANES:(hd + 1) * LANES] = (rope_b(t) * SCALE_B).astype(BF16)
        for g2 in range(4):
            kn = jnp.dot(ckv, wukk_ref[:, 2 * g2 * LANES:(2 * g2 + 2) * LANES], preferred_element_type=F32)
            for j in range(2):
                hd = 1 + 2 * g2 + j
                k_ref[rows, hd * LANES:(hd + 1) * LANES] = (kn[:, j * LANES:(j + 1) * LANES] + kpe).astype(BF16)
        for g2 in range(2):
            vb = jnp.dot(ckv, wukv_ref[:, 2 * g2 * LANES:(2 * g2 + 2) * LANES], preferred_element_type=F32)
            v_ref[rows, (1 + 2 * g2) * LANES:(3 + 2 * g2) * LANES] = vb.astype(BF16)

    _pipeline((normalise, project, expand), len(blocks))


def _odd_in_kernel(rope, x_ref, mod_ref, g_ref, w_ref, qkn_ref, *rest):
    ropes, (q_ref, k_ref, v_ref) = rest[:-3], rest[-3:]
    gq, gk = qkn_ref[0:1, :], qkn_ref[1:2, :]
    blocks = _sub_blocks(x_ref)
    held = {}

    def normalise(i):
        held[i] = _hidden(x_ref[blocks[i], :], mod_ref, g_ref)

    def project(i):
        rows, h = blocks[i], held.pop(i)
        if rope:
            ca, sa = [r[rows, :] for r in ropes]
            rope_a = lambda t: _rope_swap(t, ca, sa)
        else:
            rope_a = lambda t: t

        def proj(c0):
            return jnp.dot(h, w_ref[:, c0 * LANES:(c0 + 2) * LANES], preferred_element_type=F32)

        def tiles(p):
            return p[:, :LANES], p[:, LANES:]

        for g2 in range(2):
            for j, t in enumerate(tiles(proj(2 * g2))):
                q_ref[rows, (2 * g2 + j) * LANES:(2 * g2 + j + 1) * LANES] = (
                    rope_a(_head_norm(t, gq)) * SCALE_64).astype(BF16)
        kc, vc = tiles(proj(4))
        k_ref[rows, 0:LANES] = rope_a(_head_norm(kc, gk)).astype(BF16)
        v_ref[rows, 0:LANES] = vc.astype(BF16)
        for g2 in range(2):
            for j, t in enumerate(tiles(proj(6 + 2 * g2))):
                n = 4 + 2 * g2 + j
                q_ref[rows, n * LANES:(n + 1) * LANES] = (rope_a(t) * SCALE_64).astype(BF16)
            for j, t in enumerate(tiles(proj(10 + 2 * g2))):
                n = 1 + 2 * g2 + j
                k_ref[rows, n * LANES:(n + 1) * LANES] = rope_a(t).astype(BF16)
            v_ref[rows, (1 + 2 * g2) * LANES:(3 + 2 * g2) * LANES] = proj(14 + 2 * g2).astype(BF16)

    _pipeline((normalise, project), len(blocks))


def _rows_spec(tile, width):
    return pl.BlockSpec((None, tile, width), lambda b, t: (b, t, 0))


def _const_spec(shape, single=False):
    nd = len(shape)
    kw = dict(pipeline_mode=pl.Buffered(1)) if single else {}
    return pl.BlockSpec(shape, lambda b, t: (0,) * nd, **kw)


def _mod_spec():
    return pl.BlockSpec((None, 1, 6 * D_MODEL), lambda b, t: (b, 0, 0))


def _dense_tile(n):
    return min(DENSE_TOK, n)


def _in_proj(kernel, name, x, mod, gain, weights, ropes, widths):
    bsz, n, _ = x.shape
    tile = _dense_tile(n)
    ins = [x, mod, gain] + list(weights) + list(ropes)
    in_specs = ([_rows_spec(tile, D_MODEL), _mod_spec(), _const_spec((1, D_MODEL))]
                + [_const_spec(w.shape) for w in weights]
                + [pl.BlockSpec((tile, LANES), lambda b, t: (t, 0)) for _ in ropes])
    return pl.pallas_call(
        functools.partial(kernel, bool(ropes)),
        out_shape=[jax.ShapeDtypeStruct((bsz, n, w), BF16) for w in widths],
        grid=(bsz, n // tile),
        in_specs=in_specs,
        out_specs=[_rows_spec(tile, w) for w in widths],
        name=name,
    )(*ins)


def _half(shape, which):
    lane = _lane(shape)
    return lane < HEAD_DIM if which == "lo" else lane >= HEAD_DIM


def _query_head(q, which):
    if which is None:
        return q
    first = _first_head(q.shape)
    return jnp.where(first if which == "lo" else jnp.logical_not(first), q, jnp.zeros_like(q))


def _with_ones(v, which):
    return v if which is None else jnp.where(_half(v.shape, which), v, jnp.ones_like(v))


def _scores(q, k):
    return lax.dot_general(q, k, (((1,), (1,)), ((), ())), preferred_element_type=F32)


def _pair(acc_lo, acc_hi):
    lo = _half(acc_lo.shape, "lo")
    dens = pltpu.roll(jnp.where(lo, acc_hi, acc_lo), LANES // 2, 1)
    return jnp.where(lo, acc_lo, acc_hi) / dens


def _run_groups(groups, q_ref, guard_ref, s_ref, p_ref, acc_ref, l_ref):
    take = lambda ref, rows, t: ref[:, t * LANES:(t + 1) * LANES] if rows is None else ref[rows, t * LANES:(t + 1) * LANES]
    flat = [(g, u) for g, group in enumerate(groups) for u in range(len(group[0]))]
    sizes = [[k.shape[0] if rows is None else rows.size for k, _, rows, _ in group[1]] for group in groups]
    ring = s_ref.shape[0]

    @pl.when(guard_ref[0] != 0)
    def _():
        values, tops = {}, {}

        def score(n):
            g, u = flat[n]
            units, parts, swapped, _ = groups[g]
            qt, qh, kt, _, _ = units[u]
            q = _tile(q_ref, qt)
            q = _query_head(q, qh) if swapped else q
            off = 0
            for (k_ref, _, rows, bias), size in zip(parts, sizes[g]):
                s = _scores(q, take(k_ref, rows, kt))
                s_ref[n % ring, :, off:off + size] = s if bias is None else s + bias[...]
                off += size

        def rowmax(n):
            g, u = flat[n]
            top = None
            for c in range(sum(sizes[g]) // TOK):
                sc = s_ref[n % ring, :, c * TOK:(c + 1) * TOK]
                mc = jnp.maximum(sc[:, :LANES], sc[:, LANES:])
                top = mc if top is None else jnp.maximum(top, mc)
            m = jnp.max(top, axis=-1, keepdims=True)
            tops[n] = m if groups[g][3] is None else jnp.maximum(m, groups[g][3][u])

        def exps(n):
            g, u = flat[n]
            vh, sinks = groups[g][0][u][4], groups[g][3]
            m, den = tops.pop(n), None
            for c in range(sum(sizes[g]) // TOK):
                sc = s_ref[n % ring, :, c * TOK:(c + 1) * TOK]
                p0, p1 = jnp.exp2(sc[:, :LANES] - m), jnp.exp2(sc[:, LANES:] - m)
                p_ref[n % ring, :, c * TOK:(c + 1) * TOK] = jnp.concatenate([p0, p1], axis=1).astype(BF16)
                if vh is None:
                    den = p0 + p1 if den is None else den + (p0 + p1)
            if vh is None:
                l_ref[n] = jnp.broadcast_to(jnp.sum(den, axis=-1, keepdims=True), (TOK, LANES))
            elif sinks is not None:
                l_ref[n] = jnp.broadcast_to(jnp.exp2(sinks[u] - m), (TOK, LANES))

        def weigh(n):
            g, u = flat[n]
            _, _, _, vt, vh = groups[g][0][u]
            acc, off = None, 0
            for p, ((_, v_ref, rows, _), size) in enumerate(zip(groups[g][1], sizes[g])):
                if (g, p, vt, vh) not in values:
                    values[(g, p, vt, vh)] = _with_ones(take(v_ref, rows, vt), vh)
                part = jnp.dot(p_ref[n % ring, :, off:off + size], values[(g, p, vt, vh)],
                               preferred_element_type=F32)
                acc = part if acc is None else acc + part
                off += size
            if groups[g][3] is not None:
                acc = acc + jnp.where(_half(acc.shape, vh), 0.0, l_ref[n])
            acc_ref[n] = acc

        _pipeline((score, rowmax, exps, weigh), len(flat))

    out = [[] for _ in groups]
    for n, (g, u) in enumerate(flat):
        out[g].append(acc_ref[n] if groups[g][0][u][4] is not None else acc_ref[n] / l_ref[n])
    return out


def _key_parts(kv, self_only):
    k_lat, v_lat, k_ctx, v_ctx = kv
    return ([] if self_only else [(k_lat, v_lat, None, None)]) + [(k_ctx, v_ctx, None, None)]


def _even_attn_kernel(self_only, guard_ref, sink_ref, q_ref, *rest):
    kv, (o_ref, bias_ref, s_ref, p_ref, acc_ref, l_ref) = _kv_refs(self_only, rest)
    scratch = (s_ref, p_ref, acc_ref, l_ref)

    parts = [(kv[2], kv[3], None, None)]
    if not self_only:
        t = pl.program_id(1)
        start = pl.multiple_of(jnp.clip(t * TOK - WINDOW, 0, kv[0].shape[0] - 2 * TOK), LANES)
        qpos = t * TOK + lax.broadcasted_iota(jnp.int32, (TOK, 2 * TOK), 0)
        kpos = start + lax.broadcasted_iota(jnp.int32, (TOK, 2 * TOK), 1)
        bias_ref[...] = jnp.where(jnp.abs(qpos - kpos) <= WINDOW, 0.0, NEG_INF).astype(F32)
        parts.append((kv[0], kv[1], pl.ds(start, 2 * TOK), bias_ref))
    a_units = [(j, h, 0, 0, h) for j in range(4) for h in ("lo", "hi")]
    sinks = [jnp.full((TOK, 1), sink_ref[j if h == "lo" else 4 + j], F32) * LOG2E for j, h, _, _, _ in a_units]
    b_units = [(4 + h, None, 1 + h, 1 + h // 2, "lo" if h % 2 == 0 else "hi") for h in range(B_HEADS)]
    groups = [(a_units, parts, True, sinks), (b_units, _key_parts(kv, self_only), False, None)]
    o = _run_groups(groups, q_ref, guard_ref, *scratch)
    for g in range(2):
        for j in range(4):
            o_ref[:, (4 * g + j) * LANES:(4 * g + j + 1) * LANES] = _pair(o[g][2 * j], o[g][2 * j + 1]).astype(BF16)


def _odd_attn_kernel(lam_init, self_only, guard_ref, lam_ref, sub_ref, q_ref, *rest):
    kv, (o_ref, s_ref, p_ref, acc_ref, l_ref) = _kv_refs(self_only, rest)
    lp = lam_ref[...]
    lam = (jnp.exp(jnp.sum(lp[0:1] * lp[1:2], axis=-1, keepdims=True))
           - jnp.exp(jnp.sum(lp[2:3] * lp[3:4], axis=-1, keepdims=True)) + lam_init)

    keys = _key_parts(kv, self_only)
    c_units = [(j, h, 0, 0, h) for j in range(4) for h in ("lo", "hi")]
    d_units = [(4 + j, h, 1 + j, 1 + j, None) for j in range(D_HEADS) for h in ("lo", "hi")]
    oc, od = _run_groups([(c_units, keys, True, None), (d_units, keys, True, None)], q_ref, guard_ref,
                         s_ref, p_ref, acc_ref, l_ref)
    for j in range(4):
        o_ref[:, j * LANES:(j + 1) * LANES] = _pair(oc[2 * j], oc[2 * j + 1]).astype(BF16)
    for j in range(D_HEADS):
        diff = od[2 * j] - lam * od[2 * j + 1]
        o_ref[:, (4 + j) * LANES:(5 + j) * LANES] = (
            _rms(diff, sub_ref[...]) * (1.0 - lam_init)).astype(BF16)


def _kv_refs(self_only, refs):
    if self_only:
        return (None, None, refs[0], refs[1]), refs[2:]
    return tuple(refs[:4]), refs[4:]


def _attention(kernel, name, small, q, k_lat, v_lat, k_ctx, v_ctx, scratch=()):
    bsz, nq, qw = q.shape
    self_only = k_lat is None
    kv_spec = lambda a: pl.BlockSpec((None,) + a.shape[1:], lambda b, t: (b, 0, 0))
    kvs = [k_ctx, v_ctx] if self_only else [k_lat, v_lat, k_ctx, v_ctx]
    nk = sum(a.shape[1] for a in kvs[::2])
    return pl.pallas_call(
        functools.partial(kernel, self_only),
        out_shape=jax.ShapeDtypeStruct((bsz, nq, D_MODEL), BF16),
        grid=(bsz, nq // TOK),
        in_specs=([pl.BlockSpec(memory_space=pltpu.SMEM)] + list(small[1]) + [_rows_spec(TOK, qw)]
                  + [kv_spec(a) for a in kvs]),
        out_specs=_rows_spec(TOK, D_MODEL),
        scratch_shapes=list(scratch) + [pltpu.VMEM((RING, TOK, nk), F32), pltpu.VMEM((RING, TOK, nk), BF16),
                                        pltpu.VMEM((UNITS, TOK, LANES), F32), pltpu.VMEM((UNITS, TOK, LANES), F32)],
        name=name + ("_ctx" if self_only else ""),
    )(jnp.ones((1,), jnp.int32), *small[0], q, *kvs)


def _out_proj_kernel(o_ref, w_ref, x_ref, mod_ref, g1_ref, g2_ref, x1_ref, h_ref):
    gg = g1_ref[...] * _mod(mod_ref, 2)
    blocks = _sub_blocks(x_ref)
    ys = {}

    def project(i):
        ys[i] = jnp.dot(o_ref[blocks[i], :], w_ref[...], preferred_element_type=F32)

    def finish(i):
        rows, y = blocks[i], ys.pop(i)
        x1 = x_ref[rows, :] + y * lax.rsqrt(jnp.mean(y * y, axis=-1, keepdims=True) + EPS) * gg
        x1_ref[rows, :] = x1
        h_ref[rows, :] = _hidden(x1, mod_ref.at[:, 3 * D_MODEL:5 * D_MODEL], g2_ref)

    _pipeline((project, finish), len(blocks))


def _out_proj(o, w_out, x, mod, g1, g2):
    bsz, n, _ = o.shape
    tile = _dense_tile(n)
    return pl.pallas_call(
        _out_proj_kernel,
        out_shape=[jax.ShapeDtypeStruct((bsz, n, D_MODEL), F32),
                   jax.ShapeDtypeStruct((bsz, n, D_MODEL), BF16)],
        grid=(bsz, n // tile),
        in_specs=[_rows_spec(tile, D_MODEL), _const_spec(w_out.shape), _rows_spec(tile, D_MODEL), _mod_spec(),
                  _const_spec((1, D_MODEL)), _const_spec((1, D_MODEL))],
        out_specs=[_rows_spec(tile, D_MODEL), _rows_spec(tile, D_MODEL)],
        name="out_proj",
    )(o, w_out, x, mod, g1, g2)


def _ffn_kernel(h_ref, x_ref, mod_ref, g_ref, wi_ref, wo_ref, o_ref, acc_ref):
    h = h_ref[...]
    for c in range(FFN_HIDDEN // FFN_CHUNK):
        c0 = c * FFN_CHUNK
        g = jnp.dot(h, wi_ref[:, c0:c0 + FFN_CHUNK], preferred_element_type=F32)
        u = jnp.dot(h, wi_ref[:, FFN_HIDDEN + c0:FFN_HIDDEN + c0 + FFN_CHUNK], preferred_element_type=F32)
        a = (g * jax.nn.sigmoid(g) * u).astype(BF16)
        part = jnp.dot(a, wo_ref[c0:c0 + FFN_CHUNK, :], preferred_element_type=F32)
        if c == 0:
            acc_ref[...] = part
        else:
            acc_ref[...] += part
    gg = g_ref[...] * _mod(mod_ref, 5)
    for rows in _sub_blocks(x_ref):
        f = acc_ref[rows, :]
        o_ref[rows, :] = x_ref[rows, :] + f * lax.rsqrt(jnp.mean(f * f, axis=-1, keepdims=True) + EPS) * gg


def _ffn(h, x1, mod, g3, w_in, w_out):
    bsz, n, _ = h.shape
    tile = _dense_tile(n)
    return pl.pallas_call(
        _ffn_kernel,
        out_shape=jax.ShapeDtypeStruct((bsz, n, D_MODEL), F32),
        grid=(bsz, n // tile),
        in_specs=[_rows_spec(tile, D_MODEL), _rows_spec(tile, D_MODEL), _mod_spec(), _const_spec((1, D_MODEL)),
                  _const_spec(w_in.shape, single=True), _const_spec(w_out.shape, single=True)],
        out_specs=_rows_spec(tile, D_MODEL),
        scratch_shapes=[pltpu.VMEM((tile, D_MODEL), F32)],
        name="swiglu",
    )(h, x1, mod, g3, w_in, w_out)


def _rope_tables(seq):
    pos = jnp.arange(seq)
    rows = (pos // GRID_W).astype(F32)[:, None]
    cols = (pos % GRID_W).astype(F32)[:, None]
    lane = jnp.arange(LANES)

    def table(active, is_col, freq_idx, first, half):
        freqs = ROPE_THETA ** (-jnp.arange(half, dtype=F32) / half)
        ang = jnp.where(is_col[None, :], cols, rows) * freqs[freq_idx][None, :]
        cos = jnp.where(active[None, :], jnp.cos(ang), 1.0)
        sin = jnp.where(active[None, :], jnp.sin(ang), 0.0) * jnp.where(first, -1.0, 1.0)[None, :]
        return cos, sin

    ca, sa = table(lane >= 0, (lane % 32) >= 16, lane % 16, lane < HEAD_DIM, 16)
    e = lane - B_NOPE
    cb, sb = table((e >= 0) & (e < B_ROPE), (e // 16) == 1, e % 8, (e % 16) < 8, 8)
    return ca, sa, cb, sb


def _swap_layout(w):
    k, n = w.shape
    return w.reshape(k, n // LANES, 2, 2, 2, 16).transpose(0, 1, 4, 2, 3, 5).reshape(k, n)


def _pair_cols(w):
    k = w.shape[0]
    return w.reshape(k, 2, 4, HEAD_DIM).transpose(0, 2, 1, 3).reshape(k, 8 * HEAD_DIM)


def _pair_rows(w):
    n = w.shape[1]
    return w.reshape(2, 4, HEAD_DIM, n).transpose(1, 0, 2, 3).reshape(8 * HEAD_DIM, n)


def _even_weights(w_in, w_uq, w_ukv, w_out):
    k = w_in.shape[0]
    z = lambda r, n: jnp.zeros((r, n), w_in.dtype)
    w = jnp.concatenate([_swap_layout(jnp.concatenate([_pair_cols(w_in[:, :512]), w_in[:, 512:640]], axis=1)),
                         w_in[:, 640:1152],
                         z(k, B_NOPE), w_in[:, 1152:1184], z(k, LANES - B_NOPE - B_ROPE)], axis=1)
    uq = w_uq.reshape(B_Q_RANK, B_HEADS, B_NOPE + B_ROPE)
    uq = jnp.pad(uq, ((0, 0), (0, 0), (0, LANES - B_NOPE - B_ROPE))).reshape(B_Q_RANK, B_HEADS * LANES)
    ukv = w_ukv.reshape(B_KV_RANK, B_HEADS, B_NOPE + B_V)
    ukk = jnp.pad(ukv[:, :, :B_NOPE], ((0, 0), (0, 0), (0, LANES - B_NOPE))).reshape(B_KV_RANK, B_HEADS * LANES)
    ukvv = ukv[:, :, B_NOPE:].reshape(B_KV_RANK, B_HEADS * B_V)
    wo = jnp.concatenate([_pair_rows(w_out[:512]), w_out[512:]], axis=0)
    return [a.astype(BF16) for a in (w, uq, ukk, ukvv, wo)]


def _odd_weights(w_in, w_out):
    qc_kc = _swap_layout(jnp.concatenate([_pair_cols(w_in[:, :512]), w_in[:, 512:640]], axis=1))
    w = jnp.concatenate([qc_kc, w_in[:, 640:768], _swap_layout(w_in[:, 768:1792]), w_in[:, 1792:]], axis=1)
    wo = jnp.concatenate([_pair_rows(w_out[:512]), w_out[512:]], axis=0)
    return w.astype(BF16), wo.astype(BF16)


def kernel(x, c, ctx, c_ctx, ada_w, ada_b, norm_g, ffn_w_in, ffn_w_out, ev_w_in, ev_sink, ev_q_norm, ev_w_uq,
           ev_kv_norm, ev_w_ukv, ev_w_out, od_w_in, od_qk_norm, od_lambda, od_subln, od_w_out):
    bsz, seq, d = x.shape
    ctx_len = ctx.shape[1]
    assert d == D_MODEL and ctx_len == TOK and seq % DENSE_TOK == 0 and seq % GRID_W == 0

    mod_rows = 8 * (-(-(bsz + 1) // 8))
    s_rows = jnp.zeros((mod_rows, d), F32).at[:bsz].set(c).at[bsz].set(c_ctx)
    mod_all = _ada(s_rows, ada_w, ada_b)

    ropes = _rope_tables(seq)
    xl, xc = x, ctx.reshape(1, bsz * ctx_len, d)
    per_sample = lambda a: a.reshape(bsz, ctx_len, a.shape[-1])
    for l in range(DEPTH):
        last = l == DEPTH - 1
        mod_l, mod_c = mod_all[l, :bsz].reshape(bsz, 1, 6 * d), mod_all[l, bsz].reshape(1, 1, 6 * d)
        gains = [norm_g[l, i].reshape(1, d) for i in range(4)]
        if l % 2 == 0:
            e = l // 2
            w, uq, ukk, ukvv, wo = _even_weights(ev_w_in[e], ev_w_uq[e], ev_w_ukv[e], ev_w_out[e])
            weights = [w, uq, ukk, ukvv, ev_q_norm[e].reshape(1, -1), ev_kv_norm[e].reshape(1, -1)]
            widths = (12 * LANES, 9 * LANES, 5 * LANES)
            ql, kl, vl = _in_proj(_even_in_kernel, "even_in_proj", xl, mod_l, gains[0], weights, ropes, widths)
            qc, kc, vc = [per_sample(a) for a in
                          _in_proj(_even_in_kernel, "even_in_proj_ctx", xc, mod_c, gains[0], weights, (), widths)]
            small = ([ev_sink[e]], [pl.BlockSpec(memory_space=pltpu.SMEM)])
            attn = functools.partial(_attention, _even_attn_kernel, "even_attention", small,
                                     scratch=[pltpu.VMEM((TOK, 2 * TOK), F32)])
        else:
            i = l // 2
            w, wo = _odd_weights(od_w_in[i], od_w_out[i])
            g64 = od_qk_norm[i].reshape(2, 2, 2, 16).transpose(0, 2, 1, 3)
            weights = [w, jnp.broadcast_to(g64[:, :, None], (2, 2, 2, 2, 16)).reshape(2, LANES)]
            widths = (8 * LANES, 5 * LANES, 5 * LANES)
            ql, kl, vl = _in_proj(_odd_in_kernel, "odd_in_proj", xl, mod_l, gains[0], weights, ropes[:2], widths)
            qc, kc, vc = [per_sample(a) for a in
                          _in_proj(_odd_in_kernel, "odd_in_proj_ctx", xc, mod_c, gains[0], weights, (), widths)]
            lam_init = 0.8 - 0.6 * math.exp(-0.3 * l)
            small = ([od_lambda[i], od_subln[i].reshape(1, -1)],
                     [_const_spec((4, HEAD_DIM)), _const_spec((1, LANES))])
            attn = functools.partial(_attention, functools.partial(_odd_attn_kernel, lam_init),
                                     "odd_attention", small)
        w_in, w_out = ffn_w_in[l].astype(BF16), ffn_w_out[l].astype(BF16)
        x1, h2 = _out_proj(attn(ql, kl, vl, kc, vc), wo, xl, mod_l, gains[1], gains[2])
        xl = _ffn(h2, x1, mod_l, gains[3], w_in, w_out)
        if not last:
            oc = attn(qc, None, None, kc, vc).reshape(1, bsz * ctx_len, d)
            x1, h2 = _out_proj(oc, wo, xc, mod_c, gains[1], gains[2])
            xc = _ffn(h2, x1, mod_c, gains[3], w_in, w_out)
    return xl
```

```python
import functools
import math

import jax
import jax.numpy as jnp
from jax import lax
from jax.experimental import pallas as pl
from jax.experimental.pallas import tpu as pltpu

F32 = jnp.float32
BF16 = jnp.bfloat16

D_MODEL = 1024
DEPTH = 4
GRID_W = 64
HEAD_DIM = 64
WINDOW = 128
ROPE_THETA = 10000.0
EPS = 1e-6
NEG_INF = -1e30

A_Q_HEADS = 8
B_HEADS = 8
B_Q_RANK = 256
B_KV_RANK = 128
B_NOPE = 64
B_ROPE = 32
B_V = 64
D_HEADS = 4
FFN_HIDDEN = 2816

LANES = 128
TOK = 256
DENSE_TOK = 1024
SUB_TOK = 256
FFN_CHUNK = 256

LOG2E = math.log2(math.e)
SCALE_64 = HEAD_DIM ** -0.5 * LOG2E
SCALE_B = (B_NOPE + B_ROPE) ** -0.5 * LOG2E
UNITS = 16
RING = 8
SOFTMAX_ROWS = 64


def _rms(x, gain):
    ms = jnp.mean(x * x, axis=-1, keepdims=True)
    return x * lax.rsqrt(ms + EPS) * gain


def _lane(shape):
    return lax.broadcasted_iota(jnp.int32, shape, 1)


def _rope(x, cos, sin_signed, half):
    first = (_lane(x.shape) % (2 * half)) < half
    partner = jnp.where(first, pltpu.roll(x, LANES - half, 1), pltpu.roll(x, half, 1))
    return x * cos + partner * sin_signed


def _rope_swap(x, cos, sin_signed):
    return x * cos + pltpu.roll(x, LANES // 2, 1) * sin_signed


def _first_head(shape):
    return (_lane(shape) % HEAD_DIM) < HEAD_DIM // 2


def _head_norm(x, gain):
    lo = _first_head(x.shape)
    x2 = x * x
    s_lo = jnp.sum(jnp.where(lo, x2, 0.0), axis=-1, keepdims=True)
    s_hi = jnp.sum(jnp.where(lo, 0.0, x2), axis=-1, keepdims=True)
    inv = jnp.where(lo, lax.rsqrt(s_lo / HEAD_DIM + EPS), lax.rsqrt(s_hi / HEAD_DIM + EPS))
    return x * inv * gain


def _tile(ref, j, n=1):
    return ref[:, j * LANES:(j + n) * LANES]


def _ada_kernel(s_ref, w_ref, b_ref, o_ref):
    s = s_ref[...]
    a = (s * jax.nn.sigmoid(s)).astype(BF16)
    o_ref[...] = jnp.dot(a, w_ref[...].astype(BF16), preferred_element_type=F32) + b_ref[...]


def _ada(s_rows, ada_w, ada_b):
    rows = s_rows.shape[0]
    nblk = (6 * D_MODEL) // D_MODEL
    return pl.pallas_call(
        _ada_kernel,
        out_shape=jax.ShapeDtypeStruct((DEPTH, rows, 6 * D_MODEL), F32),
        grid=(DEPTH, nblk),
        in_specs=[
            pl.BlockSpec((rows, D_MODEL), lambda l, n: (0, 0)),
            pl.BlockSpec((None, D_MODEL, D_MODEL), lambda l, n: (l, 0, n)),
            pl.BlockSpec((None, 1, D_MODEL), lambda l, n: (l, 0, n)),
        ],
        out_specs=pl.BlockSpec((None, rows, D_MODEL), lambda l, n: (l, 0, n)),
        name="ada_mod",
    )(s_rows, ada_w, ada_b.reshape(DEPTH, 1, 6 * D_MODEL))


def _mod(mod_ref, i):
    return mod_ref[:, i * D_MODEL:(i + 1) * D_MODEL]


def _sub_blocks(ref):
    n = ref.shape[0]
    sub = min(SUB_TOK, n)
    return [slice(r, r + sub) for r in range(0, n, sub)]


def _pipeline(stages, n):
    for step in range(n + len(stages) - 1):
        for s, fn in enumerate(stages):
            if 0 <= step - s < n:
                fn(step - s)


def _hidden(x, mod_ref, g_ref):
    gs = g_ref[...] * (1.0 + _mod(mod_ref, 1))
    return (x * lax.rsqrt(jnp.mean(x * x, axis=-1, keepdims=True) + EPS) * gs + _mod(mod_ref, 0)).astype(BF16)


def _even_in_kernel(rope, x_ref, mod_ref, g_ref, w_ref, wuq_ref, wukk_ref, wukv_ref, qn_ref, kvn_ref, *rest):
    ropes, (q_ref, k_ref, v_ref) = rest[:-3], rest[-3:]
    blocks = _sub_blocks(x_ref)
    held = [{} for _ in blocks]

    def rotary(i):
        if not rope:
            return (lambda t: t), (lambda t: t)
        ca, sa, cb, sb = [r[blocks[i], :] for r in ropes]
        return (lambda t: _rope_swap(t, ca, sa)), (lambda t: _rope(t, cb, sb, 8))

    def normalise(i):
        held[i]["h"] = _hidden(x_ref[blocks[i], :], mod_ref, g_ref)

    def project(i):
        rows, h, (rope_a, rope_b) = blocks[i], held[i].pop("h"), rotary(i)

        def proj(c0, n):
            return jnp.dot(h, w_ref[:, c0 * LANES:(c0 + n) * LANES], preferred_element_type=F32)

        for g2 in range(2):
            p = proj(2 * g2, 2)
            for j in range(2):
                t = p[:, j * LANES:(j + 1) * LANES]
                q_ref[rows, (2 * g2 + j) * LANES:(2 * g2 + j + 1) * LANES] = (rope_a(t) * SCALE_64).astype(BF16)
        p = proj(4, 2)
        k_ref[rows, 0:LANES] = rope_a(p[:, :LANES]).astype(BF16)
        v_ref[rows, 0:LANES] = p[:, LANES:].astype(BF16)
        held[i]["cq"] = _rms(proj(6, 2), qn_ref[...]).astype(BF16)
        p = proj(8, 2)
        held[i]["ckv"] = _rms(p[:, :LANES], kvn_ref[...]).astype(BF16)
        held[i]["kpe"] = rope_b(p[:, LANES:])

    def expand(i):
        rows, (_, rope_b) = blocks[i], rotary(i)
        cq, ckv, kpe = held[i].pop("cq"), held[i].pop("ckv"), held[i].pop("kpe")
        for g2 in range(4):
            p = jnp.dot(cq, wuq_ref[:, 2 * g2 * LANES:(2 * g2 + 2) * LANES], preferred_element_type=F32)
            for j in range(2):
                t = p[:, j * LANES:(j + 1) * LANES]
                hd = 4 + 2 * g2 + j
                q_ref[rows, hd * LANES:(hd + 1) * LANES] = (rope_b(t) * SCALE_B).astype(BF16)
        for g2 in range(4):
            kn = jnp.dot(ckv, wukk_ref[:, 2 * g2 * LANES:(2 * g2 + 2) * LANES], preferred_element_type=F32)
            for j in range(2):
                hd = 1 + 2 * g2 + j
                k_ref[rows, hd * LANES:(hd + 1) * LANES] = (kn[:, j * LANES:(j + 1) * LANES] + kpe).astype(BF16)
        for g2 in range(2):
            vb = jnp.dot(ckv, wukv_ref[:, 2 * g2 * LANES:(2 * g2 + 2) * LANES], preferred_element_type=F32)
            v_ref[rows, (1 + 2 * g2) * LANES:(3 + 2 * g2) * LANES] = vb.astype(BF16)

    _pipeline((normalise, project, expand), len(blocks))


def _odd_in_kernel(rope, x_ref, mod_ref, g_ref, w_ref, qkn_ref, *rest):
    ropes, (q_ref, k_ref, v_ref) = rest[:-3], rest[-3:]
    gq, gk = qkn_ref[0:1, :], qkn_ref[1:2, :]
    blocks = _sub_blocks(x_ref)
    held = {}

    def normalise(i):
        held[i] = _hidden(x_ref[blocks[i], :], mod_ref, g_ref)

    def project(i):
        rows, h = blocks[i], held.pop(i)
        if rope:
            ca, sa = [r[rows, :] for r in ropes]
            rope_a = lambda t: _rope_swap(t, ca, sa)
        else:
            rope_a = lambda t: t

        def proj(c0):
            return jnp.dot(h, w_ref[:, c0 * LANES:(c0 + 2) * LANES], preferred_element_type=F32)

        def tiles(p):
            return p[:, :LANES], p[:, LANES:]

        for g2 in range(2):
            for j, t in enumerate(tiles(proj(2 * g2))):
                q_ref[rows, (2 * g2 + j) * LANES:(2 * g2 + j + 1) * LANES] = (
                    rope_a(_head_norm(t, gq)) * SCALE_64).astype(BF16)
        kc, vc = tiles(proj(4))
        k_ref[rows, 0:LANES] = rope_a(_head_norm(kc, gk)).astype(BF16)
        v_ref[rows, 0:LANES] = vc.astype(BF16)
        for g2 in range(2):
            for j, t in enumerate(tiles(proj(6 + 2 * g2))):
                n = 4 + 2 * g2 + j
                q_ref[rows, n * LANES:(n + 1) * LANES] = (rope_a(t) * SCALE_64).astype(BF16)
            for j, t in enumerate(tiles(proj(10 + 2 * g2))):
                n = 1 + 2 * g2 + j
                k_ref[rows, n * LANES:(n + 1) * LANES] = rope_a(t).astype(BF16)
            v_ref[rows, (1 + 2 * g2) * LANES:(3 + 2 * g2) * LANES] = proj(14 + 2 * g2).astype(BF16)

    _pipeline((normalise, project), len(blocks))


def _rows_spec(tile, width):
    return pl.BlockSpec((None, tile, width), lambda b, t: (b, t, 0))


def _const_spec(shape, single=False):
    nd = len(shape)
    kw = dict(pipeline_mode=pl.Buffered(1)) if single else {}
    return pl.BlockSpec(shape, lambda b, t: (0,) * nd, **kw)


def _mod_spec():
    return pl.BlockSpec((None, 1, 6 * D_MODEL), lambda b, t: (b, 0, 0))


def _dense_tile(n):
    return min(DENSE_TOK, n)


def _in_proj(kernel, name, x, mod, gain, weights, ropes, widths):
    bsz, n, _ = x.shape
    tile = _dense_tile(n)
    ins = [x, mod, gain] + list(weights) + list(ropes)
    in_specs = ([_rows_spec(tile, D_MODEL), _mod_spec(), _const_spec((1, D_MODEL))]
                + [_const_spec(w.shape) for w in weights]
                + [pl.BlockSpec((tile, LANES), lambda b, t: (t, 0)) for _ in ropes])
    return pl.pallas_call(
        functools.partial(kernel, bool(ropes)),
        out_shape=[jax.ShapeDtypeStruct((bsz, n, w), BF16) for w in widths],
        grid=(bsz, n // tile),
        in_specs=in_specs,
        out_specs=[_rows_spec(tile, w) for w in widths],
        name=name,
    )(*ins)


def _half(shape, which):
    lane = _lane(shape)
    return lane < HEAD_DIM if which == "lo" else lane >= HEAD_DIM


def _query_head(q, which):
    if which is None:
        return q
    first = _first_head(q.shape)
    return jnp.where(first if which == "lo" else jnp.logical_not(first), q, jnp.zeros_like(q))


def _with_ones(v, which):
    return v if which is None else jnp.where(_half(v.shape, which), v, jnp.ones_like(v))


def _scores(q, k):
    return lax.dot_general(q, k, (((1,), (1,)), ((), ())), preferred_element_type=F32)


def _pair(acc_lo, acc_hi):
    lo = _half(acc_lo.shape, "lo")
    dens = pltpu.roll(jnp.where(lo, acc_hi, acc_lo), LANES // 2, 1)
    return jnp.where(lo, acc_lo, acc_hi) / dens


def _run_groups(groups, q_ref, guard_ref, s_ref, p_ref, acc_ref, l_ref):
    take = lambda ref, rows, t: ref[:, t * LANES:(t + 1) * LANES] if rows is None else ref[rows, t * LANES:(t + 1) * LANES]
    sizes = [[k.shape[0] if rows is None else rows.size for k, _, rows, _ in group[1]] for group in groups]
    lams = [group[4] if len(group) > 4 else None for group in groups]
    base = [sum(len(group[0]) for group in groups[:g]) for g in range(len(groups))]
    items = []
    for g, group in enumerate(groups):
        step = 1 if lams[g] is None else 2
        items += [(g, tuple(range(u, u + step))) for u in range(0, len(group[0]), step)]
    ring = s_ref.shape[0]

    @pl.when(guard_ref[0] != 0)
    def _():
        values, tops = {}, {}

        def score(g, u):
            units, parts, swapped = groups[g][:3]
            qt, qh, kt, _, _ = units[u]
            q = _tile(q_ref, qt)
            q = _query_head(q, qh) if swapped else q
            off = 0
            for (k_ref, _, rows, bias), size in zip(parts, sizes[g]):
                s = _scores(q, take(k_ref, rows, kt))
                s_ref[(base[g] + u) % ring, :, off:off + size] = s if bias is None else s + bias[...]
                off += size

        def rowmax(g, u):
            n = base[g] + u
            tops[n] = []
            for r in range(0, TOK, SOFTMAX_ROWS):
                top = None
                for c in range(sum(sizes[g]) // TOK):
                    sc = s_ref[n % ring, r:r + SOFTMAX_ROWS, c * TOK:(c + 1) * TOK]
                    mc = jnp.maximum(sc[:, :LANES], sc[:, LANES:])
                    top = mc if top is None else jnp.maximum(top, mc)
                m = jnp.max(top, axis=-1, keepdims=True)
                tops[n].append(m if groups[g][3] is None else jnp.maximum(m, groups[g][3][u]))

        def exps(g, u):
            n = base[g] + u
            vh, sinks = groups[g][0][u][4], groups[g][3]
            for r, m in zip(range(0, TOK, SOFTMAX_ROWS), tops.pop(n)):
                rows, den = slice(r, r + SOFTMAX_ROWS), None
                for c in range(sum(sizes[g]) // TOK):
                    sc = s_ref[n % ring, rows, c * TOK:(c + 1) * TOK]
                    p0, p1 = jnp.exp2(sc[:, :LANES] - m), jnp.exp2(sc[:, LANES:] - m)
                    p_ref[n % ring, rows, c * TOK:(c + 1) * TOK] = jnp.concatenate([p0, p1], axis=1).astype(BF16)
                    if vh is None:
                        den = p0 + p1 if den is None else den + (p0 + p1)
                if vh is None:
                    l_ref[n, rows] = jnp.broadcast_to(jnp.sum(den, axis=-1, keepdims=True), (SOFTMAX_ROWS, LANES))
                elif sinks is not None:
                    l_ref[n, rows] = jnp.broadcast_to(jnp.exp2(sinks[u] - m), (SOFTMAX_ROWS, LANES))

        def weigh(g, u):
            n = base[g] + u
            _, _, _, vt, vh = groups[g][0][u]
            if lams[g] is not None:
                ratio = (lams[g] * l_ref[n - 1] / l_ref[n]).astype(BF16)
                for c in range(sum(sizes[g]) // TOK):
                    cols = slice(c * TOK, (c + 1) * TOK)
                    p1, p2 = p_ref[(n - 1) % ring, :, cols], p_ref[n % ring, :, cols]
                    a = [p1[:, t * LANES:(t + 1) * LANES] - p2[:, t * LANES:(t + 1) * LANES] * ratio for t in range(2)]
                    p_ref[n % ring, :, cols] = jnp.concatenate(a, axis=1)
            acc, off = None, 0
            for p, ((_, v_ref, rows, _), size) in enumerate(zip(groups[g][1], sizes[g])):
                if (g, p, vt, vh) not in values:
                    values[(g, p, vt, vh)] = _with_ones(take(v_ref, rows, vt), vh)
                part = jnp.dot(p_ref[n % ring, :, off:off + size], values[(g, p, vt, vh)],
                               preferred_element_type=F32)
                acc = part if acc is None else acc + part
                off += size
            if groups[g][3] is not None:
                acc = acc + jnp.where(_half(acc.shape, vh), 0.0, l_ref[n])
            if lams[g] is not None:
                acc = acc / l_ref[n - 1]
            acc_ref[n] = acc

        def each(fn):
            return lambda i: [fn(items[i][0], u) for u in items[i][1]]

        def last(fn):
            return lambda i: fn(items[i][0], items[i][1][-1])

        _pipeline((each(score), each(rowmax), each(exps), last(weigh)), len(items))

    out = []
    for g, group in enumerate(groups):
        if lams[g] is not None:
            out.append([acc_ref[base[g] + u] for u in range(1, len(group[0]), 2)])
        else:
            out.append([acc_ref[base[g] + u] if unit[4] is not None else acc_ref[base[g] + u] / l_ref[base[g] + u]
                        for u, unit in enumerate(group[0])])
    return out


def _key_parts(kv, self_only):
    k_lat, v_lat, k_ctx, v_ctx = kv
    return ([] if self_only else [(k_lat, v_lat, None, None)]) + [(k_ctx, v_ctx, None, None)]


def _even_attn_kernel(self_only, guard_ref, sink_ref, q_ref, *rest):
    kv, (o_ref, bias_ref, s_ref, p_ref, acc_ref, l_ref) = _kv_refs(self_only, rest)
    scratch = (s_ref, p_ref, acc_ref, l_ref)

    parts = [(kv[2], kv[3], None, None)]
    if not self_only:
        t = pl.program_id(1)
        start = pl.multiple_of(jnp.clip(t * TOK - WINDOW, 0, kv[0].shape[0] - 2 * TOK), LANES)
        qpos = t * TOK + lax.broadcasted_iota(jnp.int32, (TOK, 2 * TOK), 0)
        kpos = start + lax.broadcasted_iota(jnp.int32, (TOK, 2 * TOK), 1)
        bias_ref[...] = jnp.where(jnp.abs(qpos - kpos) <= WINDOW, 0.0, NEG_INF).astype(F32)
        parts.append((kv[0], kv[1], pl.ds(start, 2 * TOK), bias_ref))
    a_units = [(j, h, 0, 0, h) for j in range(4) for h in ("lo", "hi")]
    sinks = [jnp.full((SOFTMAX_ROWS, 1), sink_ref[j if h == "lo" else 4 + j], F32) * LOG2E
             for j, h, _, _, _ in a_units]
    b_units = [(4 + h, None, 1 + h, 1 + h // 2, "lo" if h % 2 == 0 else "hi") for h in range(B_HEADS)]
    groups = [(a_units, parts, True, sinks), (b_units, _key_parts(kv, self_only), False, None)]
    o = _run_groups(groups, q_ref, guard_ref, *scratch)
    for g in range(2):
        for j in range(4):
            o_ref[:, (4 * g + j) * LANES:(4 * g + j + 1) * LANES] = _pair(o[g][2 * j], o[g][2 * j + 1]).astype(BF16)


def _odd_attn_kernel(lam_init, self_only, guard_ref, lam_ref, sub_ref, q_ref, *rest):
    kv, (o_ref, s_ref, p_ref, acc_ref, l_ref) = _kv_refs(self_only, rest)
    lp = lam_ref[...]
    lam = (jnp.exp(jnp.sum(lp[0:1] * lp[1:2], axis=-1, keepdims=True))
           - jnp.exp(jnp.sum(lp[2:3] * lp[3:4], axis=-1, keepdims=True)) + lam_init)

    keys = _key_parts(kv, self_only)
    c_units = [(j, h, 0, 0, h) for j in range(4) for h in ("lo", "hi")]
    d_units = [(4 + j, h, 1 + j, 1 + j, None) for j in range(D_HEADS) for h in ("lo", "hi")]
    oc, od = _run_groups([(c_units, keys, True, None), (d_units, keys, True, None, lam)], q_ref, guard_ref,
                         s_ref, p_ref, acc_ref, l_ref)
    for j in range(4):
        o_ref[:, j * LANES:(j + 1) * LANES] = _pair(oc[2 * j], oc[2 * j + 1]).astype(BF16)
    for j in range(D_HEADS):
        o_ref[:, (4 + j) * LANES:(5 + j) * LANES] = (
            _rms(od[j], sub_ref[...]) * (1.0 - lam_init)).astype(BF16)


def _kv_refs(self_only, refs):
    if self_only:
        return (None, None, refs[0], refs[1]), refs[2:]
    return tuple(refs[:4]), refs[4:]


def _attention(kernel, name, small, q, k_lat, v_lat, k_ctx, v_ctx, scratch=()):
    bsz, nq, qw = q.shape
    self_only = k_lat is None
    kv_spec = lambda a: pl.BlockSpec((None,) + a.shape[1:], lambda b, t: (b, 0, 0))
    kvs = [k_ctx, v_ctx] if self_only else [k_lat, v_lat, k_ctx, v_ctx]
    nk = sum(a.shape[1] for a in kvs[::2])
    return pl.pallas_call(
        functools.partial(kernel, self_only),
        out_shape=jax.ShapeDtypeStruct((bsz, nq, D_MODEL), BF16),
        grid=(bsz, nq // TOK),
        in_specs=([pl.BlockSpec(memory_space=pltpu.SMEM)] + list(small[1]) + [_rows_spec(TOK, qw)]
                  + [kv_spec(a) for a in kvs]),
        out_specs=_rows_spec(TOK, D_MODEL),
        scratch_shapes=list(scratch) + [pltpu.VMEM((RING, TOK, nk), F32), pltpu.VMEM((RING, TOK, nk), BF16),
                                        pltpu.VMEM((UNITS, TOK, LANES), F32), pltpu.VMEM((UNITS, TOK, LANES), F32)],
        name=name + ("_ctx" if self_only else ""),
    )(jnp.ones((1,), jnp.int32), *small[0], q, *kvs)


def _out_proj_kernel(o_ref, w_ref, x_ref, mod_ref, g1_ref, g2_ref, x1_ref, h_ref):
    gg = g1_ref[...] * _mod(mod_ref, 2)
    blocks = _sub_blocks(x_ref)
    ys = {}

    def project(i):
        ys[i] = jnp.dot(o_ref[blocks[i], :], w_ref[...], preferred_element_type=F32)

    def finish(i):
        rows, y = blocks[i], ys.pop(i)
        x1 = x_ref[rows, :] + y * lax.rsqrt(jnp.mean(y * y, axis=-1, keepdims=True) + EPS) * gg
        x1_ref[rows, :] = x1
        h_ref[rows, :] = _hidden(x1, mod_ref.at[:, 3 * D_MODEL:5 * D_MODEL], g2_ref)

    _pipeline((project, finish), len(blocks))


def _out_proj(o, w_out, x, mod, g1, g2):
    bsz, n, _ = o.shape
    tile = _dense_tile(n)
    return pl.pallas_call(
        _out_proj_kernel,
        out_shape=[jax.ShapeDtypeStruct((bsz, n, D_MODEL), F32),
                   jax.ShapeDtypeStruct((bsz, n, D_MODEL), BF16)],
        grid=(bsz, n // tile),
        in_specs=[_rows_spec(tile, D_MODEL), _const_spec(w_out.shape), _rows_spec(tile, D_MODEL), _mod_spec(),
                  _const_spec((1, D_MODEL)), _const_spec((1, D_MODEL))],
        out_specs=[_rows_spec(tile, D_MODEL), _rows_spec(tile, D_MODEL)],
        name="out_proj",
    )(o, w_out, x, mod, g1, g2)


def _ffn_kernel(h_ref, x_ref, mod_ref, g_ref, wi_ref, wo_ref, o_ref, acc_ref):
    h = h_ref[...]
    for c in range(FFN_HIDDEN // FFN_CHUNK):
        c0 = c * FFN_CHUNK
        g = jnp.dot(h, wi_ref[:, c0:c0 + FFN_CHUNK], preferred_element_type=F32)
        u = jnp.dot(h, wi_ref[:, FFN_HIDDEN + c0:FFN_HIDDEN + c0 + FFN_CHUNK], preferred_element_type=F32)
        a = (g * jax.nn.sigmoid(g) * u).astype(BF16)
        part = jnp.dot(a, wo_ref[c0:c0 + FFN_CHUNK, :], preferred_element_type=F32)
        if c == 0:
            acc_ref[...] = part
        else:
            acc_ref[...] += part
    gg = g_ref[...] * _mod(mod_ref, 5)
    for rows in _sub_blocks(x_ref):
        f = acc_ref[rows, :]
        o_ref[rows, :] = x_ref[rows, :] + f * lax.rsqrt(jnp.mean(f * f, axis=-1, keepdims=True) + EPS) * gg


def _ffn(h, x1, mod, g3, w_in, w_out):
    bsz, n, _ = h.shape
    tile = _dense_tile(n)
    return pl.pallas_call(
        _ffn_kernel,
        out_shape=jax.ShapeDtypeStruct((bsz, n, D_MODEL), F32),
        grid=(bsz, n // tile),
        in_specs=[_rows_spec(tile, D_MODEL), _rows_spec(tile, D_MODEL), _mod_spec(), _const_spec((1, D_MODEL)),
                  _const_spec(w_in.shape, single=True), _const_spec(w_out.shape, single=True)],
        out_specs=_rows_spec(tile, D_MODEL),
        scratch_shapes=[pltpu.VMEM((tile, D_MODEL), F32)],
        name="swiglu",
    )(h, x1, mod, g3, w_in, w_out)


def _rope_tables(seq):
    pos = jnp.arange(seq)
    rows = (pos // GRID_W).astype(F32)[:, None]
    cols = (pos % GRID_W).astype(F32)[:, None]
    lane = jnp.arange(LANES)

    def table(active, is_col, freq_idx, first, half):
        freqs = ROPE_THETA ** (-jnp.arange(half, dtype=F32) / half)
        ang = jnp.where(is_col[None, :], cols, rows) * freqs[freq_idx][None, :]
        cos = jnp.where(active[None, :], jnp.cos(ang), 1.0)
        sin = jnp.where(active[None, :], jnp.sin(ang), 0.0) * jnp.where(first, -1.0, 1.0)[None, :]
        return cos, sin

    ca, sa = table(lane >= 0, (lane % 32) >= 16, lane % 16, lane < HEAD_DIM, 16)
    e = lane - B_NOPE
    cb, sb = table((e >= 0) & (e < B_ROPE), (e // 16) == 1, e % 8, (e % 16) < 8, 8)
    return ca, sa, cb, sb


def _swap_layout(w):
    k, n = w.shape
    return w.reshape(k, n // LANES, 2, 2, 2, 16).transpose(0, 1, 4, 2, 3, 5).reshape(k, n)


def _pair_cols(w):
    k = w.shape[0]
    return w.reshape(k, 2, 4, HEAD_DIM).transpose(0, 2, 1, 3).reshape(k, 8 * HEAD_DIM)


def _pair_rows(w):
    n = w.shape[1]
    return w.reshape(2, 4, HEAD_DIM, n).transpose(1, 0, 2, 3).reshape(8 * HEAD_DIM, n)


def _even_weights(w_in, w_uq, w_ukv, w_out):
    k = w_in.shape[0]
    z = lambda r, n: jnp.zeros((r, n), w_in.dtype)
    w = jnp.concatenate([_swap_layout(jnp.concatenate([_pair_cols(w_in[:, :512]), w_in[:, 512:640]], axis=1)),
                         w_in[:, 640:1152],
                         z(k, B_NOPE), w_in[:, 1152:1184], z(k, LANES - B_NOPE - B_ROPE)], axis=1)
    uq = w_uq.reshape(B_Q_RANK, B_HEADS, B_NOPE + B_ROPE)
    uq = jnp.pad(uq, ((0, 0), (0, 0), (0, LANES - B_NOPE - B_ROPE))).reshape(B_Q_RANK, B_HEADS * LANES)
    ukv = w_ukv.reshape(B_KV_RANK, B_HEADS, B_NOPE + B_V)
    ukk = jnp.pad(ukv[:, :, :B_NOPE], ((0, 0), (0, 0), (0, LANES - B_NOPE))).reshape(B_KV_RANK, B_HEADS * LANES)
    ukvv = ukv[:, :, B_NOPE:].reshape(B_KV_RANK, B_HEADS * B_V)
    wo = jnp.concatenate([_pair_rows(w_out[:512]), w_out[512:]], axis=0)
    return [a.astype(BF16) for a in (w, uq, ukk, ukvv, wo)]


def _odd_weights(w_in, w_out):
    qc_kc = _swap_layout(jnp.concatenate([_pair_cols(w_in[:, :512]), w_in[:, 512:640]], axis=1))
    w = jnp.concatenate([qc_kc, w_in[:, 640:768], _swap_layout(w_in[:, 768:1792]), w_in[:, 1792:]], axis=1)
    wo = jnp.concatenate([_pair_rows(w_out[:512]), w_out[512:]], axis=0)
    return w.astype(BF16), wo.astype(BF16)


def kernel(x, c, ctx, c_ctx, ada_w, ada_b, norm_g, ffn_w_in, ffn_w_out, ev_w_in, ev_sink, ev_q_norm, ev_w_uq,
           ev_kv_norm, ev_w_ukv, ev_w_out, od_w_in, od_qk_norm, od_lambda, od_subln, od_w_out):
    bsz, seq, d = x.shape
    ctx_len = ctx.shape[1]
    assert d == D_MODEL and ctx_len == TOK and seq % DENSE_TOK == 0 and seq % GRID_W == 0

    mod_rows = 8 * (-(-(bsz + 1) // 8))
    s_rows = jnp.zeros((mod_rows, d), F32).at[:bsz].set(c).at[bsz].set(c_ctx)
    mod_all = _ada(s_rows, ada_w, ada_b)

    ropes = _rope_tables(seq)
    xl, xc = x, ctx.reshape(1, bsz * ctx_len, d)
    per_sample = lambda a: a.reshape(bsz, ctx_len, a.shape[-1])
    for l in range(DEPTH):
        last = l == DEPTH - 1
        mod_l, mod_c = mod_all[l, :bsz].reshape(bsz, 1, 6 * d), mod_all[l, bsz].reshape(1, 1, 6 * d)
        gains = [norm_g[l, i].reshape(1, d) for i in range(4)]
        if l % 2 == 0:
            e = l // 2
            w, uq, ukk, ukvv, wo = _even_weights(ev_w_in[e], ev_w_uq[e], ev_w_ukv[e], ev_w_out[e])
            weights = [w, uq, ukk, ukvv, ev_q_norm[e].reshape(1, -1), ev_kv_norm[e].reshape(1, -1)]
            widths = (12 * LANES, 9 * LANES, 5 * LANES)
            ql, kl, vl = _in_proj(_even_in_kernel, "even_in_proj", xl, mod_l, gains[0], weights, ropes, widths)
            qc, kc, vc = [per_sample(a) for a in
                          _in_proj(_even_in_kernel, "even_in_proj_ctx", xc, mod_c, gains[0], weights, (), widths)]
            small = ([ev_sink[e]], [pl.BlockSpec(memory_space=pltpu.SMEM)])
            attn = functools.partial(_attention, _even_attn_kernel, "even_attention", small,
                                     scratch=[pltpu.VMEM((TOK, 2 * TOK), F32)])
        else:
            i = l // 2
            w, wo = _odd_weights(od_w_in[i], od_w_out[i])
            g64 = od_qk_norm[i].reshape(2, 2, 2, 16).transpose(0, 2, 1, 3)
            weights = [w, jnp.broadcast_to(g64[:, :, None], (2, 2, 2, 2, 16)).reshape(2, LANES)]
            widths = (8 * LANES, 5 * LANES, 5 * LANES)
            ql, kl, vl = _in_proj(_odd_in_kernel, "odd_in_proj", xl, mod_l, gains[0], weights, ropes[:2], widths)
            qc, kc, vc = [per_sample(a) for a in
                          _in_proj(_odd_in_kernel, "odd_in_proj_ctx", xc, mod_c, gains[0], weights, (), widths)]
            lam_init = 0.8 - 0.6 * math.exp(-0.3 * l)
            small = ([od_lambda[i], od_subln[i].reshape(1, -1)],
                     [_const_spec((4, HEAD_DIM)), _const_spec((1, LANES))])
            attn = functools.partial(_attention, functools.partial(_odd_attn_kernel, lam_init),
                                     "odd_attention", small)
        w_in, w_out = ffn_w_in[l].astype(BF16), ffn_w_out[l].astype(BF16)
        x1, h2 = _out_proj(attn(ql, kl, vl, kc, vc), wo, xl, mod_l, gains[1], gains[2])
        xl = _ffn(h2, x1, mod_l, gains[3], w_in, w_out)
        if not last:
            oc = attn(qc, None, None, kc, vc).reshape(1, bsz * ctx_len, d)
            x1, h2 = _out_proj(oc, wo, xc, mod_c, gains[1], gains[2])
            xc = _ffn(h2, x1, mod_c, gains[3], w_in, w_out)
    return xl
```

```python
import functools
import math

import jax
import jax.numpy as jnp
from jax import lax
from jax.experimental import pallas as pl
from jax.experimental.pallas import tpu as pltpu

F32 = jnp.float32
BF16 = jnp.bfloat16

D_MODEL = 1024
DEPTH = 4
GRID_W = 64
HEAD_DIM = 64
WINDOW = 128
ROPE_THETA = 10000.0
EPS = 1e-6
NEG_INF = -1e30

A_Q_HEADS = 8
B_HEADS = 8
B_Q_RANK = 256
B_KV_RANK = 128
B_NOPE = 64
B_ROPE = 32
B_V = 64
D_HEADS = 4
FFN_HIDDEN = 2816

LANES = 128
TOK = 256
DENSE_TOK = 1024
SUB_TOK = 256
FFN_CHUNK = 256

LOG2E = math.log2(math.e)
SCALE_64 = HEAD_DIM ** -0.5 * LOG2E
SCALE_B = (B_NOPE + B_ROPE) ** -0.5 * LOG2E
UNITS = 16
RING = 8
SOFTMAX_ROWS = 64


def _rms(x, gain):
    ms = jnp.mean(x * x, axis=-1, keepdims=True)
    return x * lax.rsqrt(ms + EPS) * gain


def _lane(shape):
    return lax.broadcasted_iota(jnp.int32, shape, 1)


def _rope(x, cos, sin_signed, half):
    first = (_lane(x.shape) % (2 * half)) < half
    partner = jnp.where(first, pltpu.roll(x, LANES - half, 1), pltpu.roll(x, half, 1))
    return x * cos + partner * sin_signed


def _rope_swap(x, cos, sin_signed):
    return x * cos + pltpu.roll(x, LANES // 2, 1) * sin_signed


def _first_head(shape):
    return (_lane(shape) % HEAD_DIM) < HEAD_DIM // 2


def _head_norm(x, gain):
    lo = _first_head(x.shape)
    x2 = x * x
    s_lo = jnp.sum(jnp.where(lo, x2, 0.0), axis=-1, keepdims=True)
    s_hi = jnp.sum(jnp.where(lo, 0.0, x2), axis=-1, keepdims=True)
    inv = jnp.where(lo, lax.rsqrt(s_lo / HEAD_DIM + EPS), lax.rsqrt(s_hi / HEAD_DIM + EPS))
    return x * inv * gain


def _tile(ref, j, n=1):
    return ref[:, j * LANES:(j + n) * LANES]


def _ada_kernel(s_ref, w_ref, b_ref, o_ref):
    s = s_ref[...]
    a = (s * jax.nn.sigmoid(s)).astype(BF16)
    o_ref[...] = jnp.dot(a, w_ref[...].astype(BF16), preferred_element_type=F32) + b_ref[...]


def _ada(s_rows, ada_w, ada_b):
    rows = s_rows.shape[0]
    nblk = (6 * D_MODEL) // D_MODEL
    return pl.pallas_call(
        _ada_kernel,
        out_shape=jax.ShapeDtypeStruct((DEPTH, rows, 6 * D_MODEL), F32),
        grid=(DEPTH, nblk),
        in_specs=[
            pl.BlockSpec((rows, D_MODEL), lambda l, n: (0, 0)),
            pl.BlockSpec((None, D_MODEL, D_MODEL), lambda l, n: (l, 0, n)),
            pl.BlockSpec((None, 1, D_MODEL), lambda l, n: (l, 0, n)),
        ],
        out_specs=pl.BlockSpec((None, rows, D_MODEL), lambda l, n: (l, 0, n)),
        name="ada_mod",
    )(s_rows, ada_w, ada_b.reshape(DEPTH, 1, 6 * D_MODEL))


def _mod(mod_ref, i):
    return mod_ref[:, i * D_MODEL:(i + 1) * D_MODEL]


def _sub_blocks(ref):
    n = ref.shape[0]
    sub = min(SUB_TOK, n)
    return [slice(r, r + sub) for r in range(0, n, sub)]


def _pipeline(stages, n):
    for step in range(n + len(stages) - 1):
        for s, fn in enumerate(stages):
            if 0 <= step - s < n:
                fn(step - s)


def _hidden(x, mod_ref, g_ref):
    gs = g_ref[...] * (1.0 + _mod(mod_ref, 1))
    return (x * lax.rsqrt(jnp.mean(x * x, axis=-1, keepdims=True) + EPS) * gs + _mod(mod_ref, 0)).astype(BF16)


def _even_in_kernel(rope, x_ref, mod_ref, g_ref, w_ref, wuq_ref, wukk_ref, wukv_ref, qn_ref, kvn_ref, *rest):
    ropes, (q_ref, k_ref, v_ref) = rest[:-3], rest[-3:]
    blocks = _sub_blocks(x_ref)
    held = [{} for _ in blocks]

    def rotary(i):
        if not rope:
            return (lambda t: t), (lambda t: t)
        ca, sa, cb, sb = [r[blocks[i], :] for r in ropes]
        return (lambda t: _rope_swap(t, ca, sa)), (lambda t: _rope(t, cb, sb, 8))

    def normalise(i):
        held[i]["h"] = _hidden(x_ref[blocks[i], :], mod_ref, g_ref)

    def project(i):
        rows, h, (rope_a, rope_b) = blocks[i], held[i].pop("h"), rotary(i)

        def proj(c0, n):
            return jnp.dot(h, w_ref[:, c0 * LANES:(c0 + n) * LANES], preferred_element_type=F32)

        for g2 in range(2):
            p = proj(2 * g2, 2)
            for j in range(2):
                t = p[:, j * LANES:(j + 1) * LANES]
                q_ref[rows, (2 * g2 + j) * LANES:(2 * g2 + j + 1) * LANES] = (rope_a(t) * SCALE_64).astype(BF16)
        p = proj(4, 2)
        k_ref[rows, 0:LANES] = rope_a(p[:, :LANES]).astype(BF16)
        v_ref[rows, 0:LANES] = p[:, LANES:].astype(BF16)
        held[i]["cq"] = _rms(proj(6, 2), qn_ref[...]).astype(BF16)
        p = proj(8, 2)
        held[i]["ckv"] = _rms(p[:, :LANES], kvn_ref[...]).astype(BF16)
        held[i]["kpe"] = rope_b(p[:, LANES:])

    def expand(i):
        rows, (_, rope_b) = blocks[i], rotary(i)
        cq, ckv, kpe = held[i].pop("cq"), held[i].pop("ckv"), held[i].pop("kpe")
        for g2 in range(4):
            p = jnp.dot(cq, wuq_ref[:, 2 * g2 * LANES:(2 * g2 + 2) * LANES], preferred_element_type=F32)
            for j in range(2):
                t = p[:, j * LANES:(j + 1) * LANES]
                hd = 4 + 2 * g2 + j
                q_ref[rows, hd * LANES:(hd + 1) * LANES] = (rope_b(t) * SCALE_B).astype(BF16)
        for g2 in range(4):
            kn = jnp.dot(ckv, wukk_ref[:, 2 * g2 * LANES:(2 * g2 + 2) * LANES], preferred_element_type=F32)
            for j in range(2):
                hd = 1 + 2 * g2 + j
                k_ref[rows, hd * LANES:(hd + 1) * LANES] = (kn[:, j * LANES:(j + 1) * LANES] + kpe).astype(BF16)
        for g2 in range(2):
            vb = jnp.dot(ckv, wukv_ref[:, 2 * g2 * LANES:(2 * g2 + 2) * LANES], preferred_element_type=F32)
            v_ref[rows, (1 + 2 * g2) * LANES:(3 + 2 * g2) * LANES] = vb.astype(BF16)

    _pipeline((normalise, project, expand), len(blocks))


def _odd_in_kernel(rope, x_ref, mod_ref, g_ref, w_ref, qkn_ref, *rest):
    ropes, (q_ref, k_ref, v_ref) = rest[:-3], rest[-3:]
    gq, gk = qkn_ref[0:1, :], qkn_ref[1:2, :]
    blocks = _sub_blocks(x_ref)
    held = {}

    def normalise(i):
        held[i] = _hidden(x_ref[blocks[i], :], mod_ref, g_ref)

    def project(i):
        rows, h = blocks[i], held.pop(i)
        if rope:
            ca, sa = [r[rows, :] for r in ropes]
            rope_a = lambda t: _rope_swap(t, ca, sa)
        else:
            rope_a = lambda t: t

        def proj(c0):
            return jnp.dot(h, w_ref[:, c0 * LANES:(c0 + 2) * LANES], preferred_element_type=F32)

        def tiles(p):
            return p[:, :LANES], p[:, LANES:]

        for g2 in range(2):
            for j, t in enumerate(tiles(proj(2 * g2))):
                q_ref[rows, (2 * g2 + j) * LANES:(2 * g2 + j + 1) * LANES] = (
                    rope_a(_head_norm(t, gq)) * SCALE_64).astype(BF16)
        kc, vc = tiles(proj(4))
        k_ref[rows, 0:LANES] = rope_a(_head_norm(kc, gk)).astype(BF16)
        v_ref[rows, 0:LANES] = vc.astype(BF16)
        for g2 in range(2):
            for j, t in enumerate(tiles(proj(6 + 2 * g2))):
                n = 4 + 2 * g2 + j
                q_ref[rows, n * LANES:(n + 1) * LANES] = (rope_a(t) * SCALE_64).astype(BF16)
            for j, t in enumerate(tiles(proj(10 + 2 * g2))):
                n = 1 + 2 * g2 + j
                k_ref[rows, n * LANES:(n + 1) * LANES] = rope_a(t).astype(BF16)
            v_ref[rows, (1 + 2 * g2) * LANES:(3 + 2 * g2) * LANES] = proj(14 + 2 * g2).astype(BF16)

    _pipeline((normalise, project), len(blocks))


def _rows_spec(tile, width):
    return pl.BlockSpec((None, tile, width), lambda b, t: (b, t, 0))


def _const_spec(shape, single=False):
    nd = len(shape)
    kw = dict(pipeline_mode=pl.Buffered(1)) if single else {}
    return pl.BlockSpec(shape, lambda b, t: (0,) * nd, **kw)


def _mod_spec():
    return pl.BlockSpec((None, 1, 6 * D_MODEL), lambda b, t: (b, 0, 0))


def _dense_tile(n):
    return min(DENSE_TOK, n)


def _in_proj(kernel, name, x, mod, gain, weights, ropes, widths):
    bsz, n, _ = x.shape
    tile = _dense_tile(n)
    ins = [x, mod, gain] + list(weights) + list(ropes)
    in_specs = ([_rows_spec(tile, D_MODEL), _mod_spec(), _const_spec((1, D_MODEL))]
                + [_const_spec(w.shape) for w in weights]
                + [pl.BlockSpec((tile, LANES), lambda b, t: (t, 0)) for _ in ropes])
    return pl.pallas_call(
        functools.partial(kernel, bool(ropes)),
        out_shape=[jax.ShapeDtypeStruct((bsz, n, w), BF16) for w in widths],
        grid=(bsz, n // tile),
        in_specs=in_specs,
        out_specs=[_rows_spec(tile, w) for w in widths],
        name=name,
    )(*ins)


def _half(shape, which):
    lane = _lane(shape)
    return lane < HEAD_DIM if which == "lo" else lane >= HEAD_DIM


def _query_head(q, which):
    if which is None:
        return q
    first = _first_head(q.shape)
    return jnp.where(first if which == "lo" else jnp.logical_not(first), q, jnp.zeros_like(q))


def _with_ones(v, which):
    return v if which is None else jnp.where(_half(v.shape, which), v, jnp.ones_like(v))


def _scores(q, k):
    return lax.dot_general(q, k, (((1,), (1,)), ((), ())), preferred_element_type=F32)


def _pair(acc_lo, acc_hi):
    lo = _half(acc_lo.shape, "lo")
    dens = pltpu.roll(jnp.where(lo, acc_hi, acc_lo), LANES // 2, 1)
    return jnp.where(lo, acc_lo, acc_hi) / dens


def _run_groups(groups, q_ref, guard_ref, s_ref, p_ref, acc_ref, l_ref):
    take = lambda ref, rows, t: ref[:, t * LANES:(t + 1) * LANES] if rows is None else ref[rows, t * LANES:(t + 1) * LANES]
    sizes = [[k.shape[0] if rows is None else rows.size for k, _, rows, _ in group[1]] for group in groups]
    base = [sum(len(group[0]) for group in groups[:g]) for g in range(len(groups))]
    items = [(g, u) for g, group in enumerate(groups) for u in range(len(group[0]))]
    ring = s_ref.shape[0]

    @pl.when(guard_ref[0] != 0)
    def _():
        values, tops = {}, {}

        def score(g, u):
            units, parts, swapped = groups[g][:3]
            qt, qh, kt, _, _ = units[u]
            q = _tile(q_ref, qt)
            q = _query_head(q, qh) if swapped else q
            off = 0
            for (k_ref, _, rows, bias), size in zip(parts, sizes[g]):
                s = _scores(q, take(k_ref, rows, kt))
                s_ref[(base[g] + u) % ring, :, off:off + size] = s if bias is None else s + bias[...]
                off += size

        def rowmax(g, u):
            n = base[g] + u
            tops[n] = []
            for r in range(0, TOK, SOFTMAX_ROWS):
                top = None
                for c in range(sum(sizes[g]) // TOK):
                    sc = s_ref[n % ring, r:r + SOFTMAX_ROWS, c * TOK:(c + 1) * TOK]
                    mc = jnp.maximum(sc[:, :LANES], sc[:, LANES:])
                    top = mc if top is None else jnp.maximum(top, mc)
                m = jnp.max(top, axis=-1, keepdims=True)
                tops[n].append(m if groups[g][3] is None else jnp.maximum(m, groups[g][3][u]))

        def exps(g, u):
            n = base[g] + u
            vh, sinks = groups[g][0][u][4], groups[g][3]
            for r, m in zip(range(0, TOK, SOFTMAX_ROWS), tops.pop(n)):
                rows, den = slice(r, r + SOFTMAX_ROWS), None
                for c in range(sum(sizes[g]) // TOK):
                    sc = s_ref[n % ring, rows, c * TOK:(c + 1) * TOK]
                    p0, p1 = jnp.exp2(sc[:, :LANES] - m), jnp.exp2(sc[:, LANES:] - m)
                    p_ref[n % ring, rows, c * TOK:(c + 1) * TOK] = jnp.concatenate([p0, p1], axis=1).astype(BF16)
                    if vh is None:
                        den = p0 + p1 if den is None else den + (p0 + p1)
                if vh is None:
                    l_ref[n, rows] = jnp.broadcast_to(jnp.sum(den, axis=-1, keepdims=True), (SOFTMAX_ROWS, LANES))
                elif sinks is not None:
                    l_ref[n, rows] = jnp.broadcast_to(jnp.exp2(sinks[u] - m), (SOFTMAX_ROWS, LANES))

        def weigh(g, u):
            n = base[g] + u
            _, _, _, vt, vh = groups[g][0][u]
            acc, off = None, 0
            for p, ((_, v_ref, rows, _), size) in enumerate(zip(groups[g][1], sizes[g])):
                if (g, p, vt, vh) not in values:
                    values[(g, p, vt, vh)] = _with_ones(take(v_ref, rows, vt), vh)
                part = jnp.dot(p_ref[n % ring, :, off:off + size], values[(g, p, vt, vh)],
                               preferred_element_type=F32)
                acc = part if acc is None else acc + part
                off += size
            if groups[g][3] is not None:
                acc = acc + jnp.where(_half(acc.shape, vh), 0.0, l_ref[n])
            acc_ref[n] = acc

        _pipeline([lambda i, fn=fn: fn(*items[i]) for fn in (score, rowmax, exps, weigh)], len(items))

    return [[acc_ref[base[g] + u] if unit[4] is not None else acc_ref[base[g] + u] / l_ref[base[g] + u]
             for u, unit in enumerate(group[0])] for g, group in enumerate(groups)]


def _key_parts(kv, self_only):
    k_lat, v_lat, k_ctx, v_ctx = kv
    return ([] if self_only else [(k_lat, v_lat, None, None)]) + [(k_ctx, v_ctx, None, None)]


def _even_attn_kernel(self_only, guard_ref, sink_ref, q_ref, *rest):
    kv, (o_ref, bias_ref, s_ref, p_ref, acc_ref, l_ref) = _kv_refs(self_only, rest)
    scratch = (s_ref, p_ref, acc_ref, l_ref)

    parts = [(kv[2], kv[3], None, None)]
    if not self_only:
        t = pl.program_id(1)
        start = pl.multiple_of(jnp.clip(t * TOK - WINDOW, 0, kv[0].shape[0] - 2 * TOK), LANES)
        qpos = t * TOK + lax.broadcasted_iota(jnp.int32, (TOK, 2 * TOK), 0)
        kpos = start + lax.broadcasted_iota(jnp.int32, (TOK, 2 * TOK), 1)
        bias_ref[...] = jnp.where(jnp.abs(qpos - kpos) <= WINDOW, 0.0, NEG_INF).astype(F32)
        parts.append((kv[0], kv[1], pl.ds(start, 2 * TOK), bias_ref))
    a_units = [(j, h, 0, 0, h) for j in range(4) for h in ("lo", "hi")]
    sinks = [jnp.full((SOFTMAX_ROWS, 1), sink_ref[j if h == "lo" else 4 + j], F32) * LOG2E
             for j, h, _, _, _ in a_units]
    b_units = [(4 + h, None, 1 + h, 1 + h // 2, "lo" if h % 2 == 0 else "hi") for h in range(B_HEADS)]
    groups = [(a_units, parts, True, sinks), (b_units, _key_parts(kv, self_only), False, None)]
    o = _run_groups(groups, q_ref, guard_ref, *scratch)
    for g in range(2):
        for j in range(4):
            o_ref[:, (4 * g + j) * LANES:(4 * g + j + 1) * LANES] = _pair(o[g][2 * j], o[g][2 * j + 1]).astype(BF16)


def _odd_attn_kernel(lam_init, self_only, guard_ref, lam_ref, sub_ref, q_ref, *rest):
    kv, (o_ref, s_ref, p_ref, acc_ref, l_ref) = _kv_refs(self_only, rest)
    lp = lam_ref[...]
    lam = (jnp.exp(jnp.sum(lp[0:1] * lp[1:2], axis=-1, keepdims=True))
           - jnp.exp(jnp.sum(lp[2:3] * lp[3:4], axis=-1, keepdims=True)) + lam_init)

    keys = _key_parts(kv, self_only)
    c_units = [(j, h, 0, 0, h) for j in range(4) for h in ("lo", "hi")]
    d_units = [(4 + j, h, 1 + j, 1 + j, None) for j in range(D_HEADS) for h in ("lo", "hi")]
    oc, od = _run_groups([(c_units, keys, True, None), (d_units, keys, True, None)], q_ref, guard_ref,
                         s_ref, p_ref, acc_ref, l_ref)
    for j in range(4):
        o_ref[:, j * LANES:(j + 1) * LANES] = _pair(oc[2 * j], oc[2 * j + 1]).astype(BF16)
    for j in range(D_HEADS):
        diff = od[2 * j] - lam * od[2 * j + 1]
        o_ref[:, (4 + j) * LANES:(5 + j) * LANES] = (
            _rms(diff, sub_ref[...]) * (1.0 - lam_init)).astype(BF16)


def _kv_refs(self_only, refs):
    if self_only:
        return (None, None, refs[0], refs[1]), refs[2:]
    return tuple(refs[:4]), refs[4:]


def _attention(kernel, name, small, q, k_lat, v_lat, k_ctx, v_ctx, scratch=()):
    bsz, nq, qw = q.shape
    self_only = k_lat is None
    kv_spec = lambda a: pl.BlockSpec((None,) + a.shape[1:], lambda b, t: (b, 0, 0))
    kvs = [k_ctx, v_ctx] if self_only else [k_lat, v_lat, k_ctx, v_ctx]
    nk = sum(a.shape[1] for a in kvs[::2])
    return pl.pallas_call(
        functools.partial(kernel, self_only),
        out_shape=jax.ShapeDtypeStruct((bsz, nq, D_MODEL), BF16),
        grid=(bsz, nq // TOK),
        in_specs=([pl.BlockSpec(memory_space=pltpu.SMEM)] + list(small[1]) + [_rows_spec(TOK, qw)]
                  + [kv_spec(a) for a in kvs]),
        out_specs=_rows_spec(TOK, D_MODEL),
        scratch_shapes=list(scratch) + [pltpu.VMEM((RING, TOK, nk), F32), pltpu.VMEM((RING, TOK, nk), BF16),
                                        pltpu.VMEM((UNITS, TOK, LANES), F32), pltpu.VMEM((UNITS, TOK, LANES), F32)],
        name=name + ("_ctx" if self_only else ""),
    )(jnp.ones((1,), jnp.int32), *small[0], q, *kvs)


def _mix_ffn_kernel(o_ref, wm_ref, x_ref, mod_ref, g_ref, wi_ref, wo_ref, out_ref, h_ref, acc_ref):
    gg = g_ref[0:1, :] * _mod(mod_ref, 2)
    blocks = _sub_blocks(x_ref)
    ys = {}

    def project(i):
        ys[i] = jnp.dot(o_ref[blocks[i], :], wm_ref[...], preferred_element_type=F32)

    def finish(i):
        rows, y = blocks[i], ys.pop(i)
        x1 = x_ref[rows, :] + y * lax.rsqrt(jnp.mean(y * y, axis=-1, keepdims=True) + EPS) * gg
        out_ref[rows, :] = x1
        h_ref[rows, :] = _hidden(x1, mod_ref.at[:, 3 * D_MODEL:5 * D_MODEL], g_ref.at[1:2, :])

    _pipeline((project, finish), len(blocks))

    h = h_ref[...]
    for c in range(FFN_HIDDEN // FFN_CHUNK):
        c0 = c * FFN_CHUNK
        g = jnp.dot(h, wi_ref[:, c0:c0 + FFN_CHUNK], preferred_element_type=F32)
        u = jnp.dot(h, wi_ref[:, FFN_HIDDEN + c0:FFN_HIDDEN + c0 + FFN_CHUNK], preferred_element_type=F32)
        a = (g * jax.nn.sigmoid(g) * u).astype(BF16)
        part = jnp.dot(a, wo_ref[c0:c0 + FFN_CHUNK, :], preferred_element_type=F32)
        if c == 0:
            acc_ref[...] = part
        else:
            acc_ref[...] += part
    gg = g_ref[2:3, :] * _mod(mod_ref, 5)
    for rows in blocks:
        f = acc_ref[rows, :]
        out_ref[rows, :] = out_ref[rows, :] + f * lax.rsqrt(jnp.mean(f * f, axis=-1, keepdims=True) + EPS) * gg


def _mix_ffn(o, w_mix, x, mod, gains, w_in, w_out):
    bsz, n, _ = o.shape
    tile = _dense_tile(n)
    return pl.pallas_call(
        _mix_ffn_kernel,
        out_shape=jax.ShapeDtypeStruct((bsz, n, D_MODEL), F32),
        grid=(bsz, n // tile),
        in_specs=[_rows_spec(tile, D_MODEL), _const_spec(w_mix.shape, single=True), _rows_spec(tile, D_MODEL),
                  _mod_spec(), _const_spec(gains.shape), _const_spec(w_in.shape, single=True),
                  _const_spec(w_out.shape, single=True)],
        out_specs=_rows_spec(tile, D_MODEL),
        scratch_shapes=[pltpu.VMEM((tile, D_MODEL), BF16), pltpu.VMEM((tile, D_MODEL), F32)],
        name="mix_swiglu",
    )(o, w_mix, x, mod, gains, w_in, w_out)


def _rope_tables(seq):
    pos = jnp.arange(seq)
    rows = (pos // GRID_W).astype(F32)[:, None]
    cols = (pos % GRID_W).astype(F32)[:, None]
    lane = jnp.arange(LANES)

    def table(active, is_col, freq_idx, first, half):
        freqs = ROPE_THETA ** (-jnp.arange(half, dtype=F32) / half)
        ang = jnp.where(is_col[None, :], cols, rows) * freqs[freq_idx][None, :]
        cos = jnp.where(active[None, :], jnp.cos(ang), 1.0)
        sin = jnp.where(active[None, :], jnp.sin(ang), 0.0) * jnp.where(first, -1.0, 1.0)[None, :]
        return cos, sin

    ca, sa = table(lane >= 0, (lane % 32) >= 16, lane % 16, lane < HEAD_DIM, 16)
    e = lane - B_NOPE
    cb, sb = table((e >= 0) & (e < B_ROPE), (e // 16) == 1, e % 8, (e % 16) < 8, 8)
    return ca, sa, cb, sb


def _swap_layout(w):
    k, n = w.shape
    return w.reshape(k, n // LANES, 2, 2, 2, 16).transpose(0, 1, 4, 2, 3, 5).reshape(k, n)


def _pair_cols(w):
    k = w.shape[0]
    return w.reshape(k, 2, 4, HEAD_DIM).transpose(0, 2, 1, 3).reshape(k, 8 * HEAD_DIM)


def _pair_rows(w):
    n = w.shape[1]
    return w.reshape(2, 4, HEAD_DIM, n).transpose(1, 0, 2, 3).reshape(8 * HEAD_DIM, n)


def _even_weights(w_in, w_uq, w_ukv, w_out):
    k = w_in.shape[0]
    z = lambda r, n: jnp.zeros((r, n), w_in.dtype)
    w = jnp.concatenate([_swap_layout(jnp.concatenate([_pair_cols(w_in[:, :512]), w_in[:, 512:640]], axis=1)),
                         w_in[:, 640:1152],
                         z(k, B_NOPE), w_in[:, 1152:1184], z(k, LANES - B_NOPE - B_ROPE)], axis=1)
    uq = w_uq.reshape(B_Q_RANK, B_HEADS, B_NOPE + B_ROPE)
    uq = jnp.pad(uq, ((0, 0), (0, 0), (0, LANES - B_NOPE - B_ROPE))).reshape(B_Q_RANK, B_HEADS * LANES)
    ukv = w_ukv.reshape(B_KV_RANK, B_HEADS, B_NOPE + B_V)
    ukk = jnp.pad(ukv[:, :, :B_NOPE], ((0, 0), (0, 0), (0, LANES - B_NOPE))).reshape(B_KV_RANK, B_HEADS * LANES)
    ukvv = ukv[:, :, B_NOPE:].reshape(B_KV_RANK, B_HEADS * B_V)
    wo = jnp.concatenate([_pair_rows(w_out[:512]), w_out[512:]], axis=0)
    return [a.astype(BF16) for a in (w, uq, ukk, ukvv, wo)]


def _odd_weights(w_in, w_out):
    qc_kc = _swap_layout(jnp.concatenate([_pair_cols(w_in[:, :512]), w_in[:, 512:640]], axis=1))
    w = jnp.concatenate([qc_kc, w_in[:, 640:768], _swap_layout(w_in[:, 768:1792]), w_in[:, 1792:]], axis=1)
    wo = jnp.concatenate([_pair_rows(w_out[:512]), w_out[512:]], axis=0)
    return w.astype(BF16), wo.astype(BF16)


def kernel(x, c, ctx, c_ctx, ada_w, ada_b, norm_g, ffn_w_in, ffn_w_out, ev_w_in, ev_sink, ev_q_norm, ev_w_uq,
           ev_kv_norm, ev_w_ukv, ev_w_out, od_w_in, od_qk_norm, od_lambda, od_subln, od_w_out):
    bsz, seq, d = x.shape
    ctx_len = ctx.shape[1]
    assert d == D_MODEL and ctx_len == TOK and seq % DENSE_TOK == 0 and seq % GRID_W == 0

    mod_rows = 8 * (-(-(bsz + 1) // 8))
    s_rows = jnp.zeros((mod_rows, d), F32).at[:bsz].set(c).at[bsz].set(c_ctx)
    mod_all = _ada(s_rows, ada_w, ada_b)

    ropes = _rope_tables(seq)
    xl, xc = x, ctx.reshape(1, bsz * ctx_len, d)
    per_sample = lambda a: a.reshape(bsz, ctx_len, a.shape[-1])
    for l in range(DEPTH):
        last = l == DEPTH - 1
        mod_l, mod_c = mod_all[l, :bsz].reshape(bsz, 1, 6 * d), mod_all[l, bsz].reshape(1, 1, 6 * d)
        gains = [norm_g[l, i].reshape(1, d) for i in range(4)]
        if l % 2 == 0:
            e = l // 2
            w, uq, ukk, ukvv, wo = _even_weights(ev_w_in[e], ev_w_uq[e], ev_w_ukv[e], ev_w_out[e])
            weights = [w, uq, ukk, ukvv, ev_q_norm[e].reshape(1, -1), ev_kv_norm[e].reshape(1, -1)]
            widths = (12 * LANES, 9 * LANES, 5 * LANES)
            ql, kl, vl = _in_proj(_even_in_kernel, "even_in_proj", xl, mod_l, gains[0], weights, ropes, widths)
            qc, kc, vc = [per_sample(a) for a in
                          _in_proj(_even_in_kernel, "even_in_proj_ctx", xc, mod_c, gains[0], weights, (), widths)]
            small = ([ev_sink[e]], [pl.BlockSpec(memory_space=pltpu.SMEM)])
            attn = functools.partial(_attention, _even_attn_kernel, "even_attention", small,
                                     scratch=[pltpu.VMEM((TOK, 2 * TOK), F32)])
        else:
            i = l // 2
            w, wo = _odd_weights(od_w_in[i], od_w_out[i])
            g64 = od_qk_norm[i].reshape(2, 2, 2, 16).transpose(0, 2, 1, 3)
            weights = [w, jnp.broadcast_to(g64[:, :, None], (2, 2, 2, 2, 16)).reshape(2, LANES)]
            widths = (8 * LANES, 5 * LANES, 5 * LANES)
            ql, kl, vl = _in_proj(_odd_in_kernel, "odd_in_proj", xl, mod_l, gains[0], weights, ropes[:2], widths)
            qc, kc, vc = [per_sample(a) for a in
                          _in_proj(_odd_in_kernel, "odd_in_proj_ctx", xc, mod_c, gains[0], weights, (), widths)]
            lam_init = 0.8 - 0.6 * math.exp(-0.3 * l)
            small = ([od_lambda[i], od_subln[i].reshape(1, -1)],
                     [_const_spec((4, HEAD_DIM)), _const_spec((1, LANES))])
            attn = functools.partial(_attention, functools.partial(_odd_attn_kernel, lam_init),
                                     "odd_attention", small)
        w_in, w_out = ffn_w_in[l].astype(BF16), ffn_w_out[l].astype(BF16)
        xl = _mix_ffn(attn(ql, kl, vl, kc, vc), wo, xl, mod_l, norm_g[l, 1:], w_in, w_out)
        if not last:
            oc = attn(qc, None, None, kc, vc).reshape(1, bsz * ctx_len, d)
            xc = _mix_ffn(oc, wo, xc, mod_c, norm_g[l, 1:], w_in, w_out)
    return xl
```

```python
import functools
import math

import jax
import jax.numpy as jnp
from jax import lax
from jax.experimental import pallas as pl
from jax.experimental.pallas import tpu as pltpu

F32 = jnp.float32
BF16 = jnp.bfloat16

D_MODEL = 1024
DEPTH = 4
GRID_W = 64
HEAD_DIM = 64
WINDOW = 128
ROPE_THETA = 10000.0
EPS = 1e-6
NEG_INF = -1e30

A_Q_HEADS = 8
B_HEADS = 8
B_Q_RANK = 256
B_KV_RANK = 128
B_NOPE = 64
B_ROPE = 32
B_V = 64
D_HEADS = 4
FFN_HIDDEN = 2816

LANES = 128
TOK = 256
DENSE_TOK = 1024
SUB_TOK = 256
FFN_CHUNK = 256

LOG2E = math.log2(math.e)
SCALE_64 = HEAD_DIM ** -0.5 * LOG2E
SCALE_B = (B_NOPE + B_ROPE) ** -0.5 * LOG2E
UNITS = 16
RING = 8
SOFTMAX_ROWS = 64


def _rms(x, gain):
    ms = jnp.mean(x * x, axis=-1, keepdims=True)
    return x * lax.rsqrt(ms + EPS) * gain


def _lane(shape):
    return lax.broadcasted_iota(jnp.int32, shape, 1)


def _rope(x, cos, sin_signed, half):
    first = (_lane(x.shape) % (2 * half)) < half
    partner = jnp.where(first, pltpu.roll(x, LANES - half, 1), pltpu.roll(x, half, 1))
    return x * cos + partner * sin_signed


def _rope_swap(x, cos, sin_signed):
    return x * cos + pltpu.roll(x, LANES // 2, 1) * sin_signed


def _first_head(shape):
    return (_lane(shape) % HEAD_DIM) < HEAD_DIM // 2


def _head_norm(x, gain):
    lo = _first_head(x.shape)
    x2 = x * x
    s_lo = jnp.sum(jnp.where(lo, x2, 0.0), axis=-1, keepdims=True)
    s_hi = jnp.sum(jnp.where(lo, 0.0, x2), axis=-1, keepdims=True)
    inv = jnp.where(lo, lax.rsqrt(s_lo / HEAD_DIM + EPS), lax.rsqrt(s_hi / HEAD_DIM + EPS))
    return x * inv * gain


def _tile(ref, j, n=1):
    return ref[:, j * LANES:(j + n) * LANES]


def _ada_kernel(s_ref, w_ref, b_ref, o_ref):
    s = s_ref[...]
    a = (s * jax.nn.sigmoid(s)).astype(BF16)
    o_ref[...] = jnp.dot(a, w_ref[...].astype(BF16), preferred_element_type=F32) + b_ref[...]


def _ada(s_rows, ada_w, ada_b):
    rows = s_rows.shape[0]
    nblk = (6 * D_MODEL) // D_MODEL
    return pl.pallas_call(
        _ada_kernel,
        out_shape=jax.ShapeDtypeStruct((DEPTH, rows, 6 * D_MODEL), F32),
        grid=(DEPTH, nblk),
        in_specs=[
            pl.BlockSpec((rows, D_MODEL), lambda l, n: (0, 0)),
            pl.BlockSpec((None, D_MODEL, D_MODEL), lambda l, n: (l, 0, n)),
            pl.BlockSpec((None, 1, D_MODEL), lambda l, n: (l, 0, n)),
        ],
        out_specs=pl.BlockSpec((None, rows, D_MODEL), lambda l, n: (l, 0, n)),
        name="ada_mod",
    )(s_rows, ada_w, ada_b.reshape(DEPTH, 1, 6 * D_MODEL))


def _mod(mod_ref, i):
    return mod_ref[:, i * D_MODEL:(i + 1) * D_MODEL]


def _sub_blocks(ref):
    n = ref.shape[0]
    sub = min(SUB_TOK, n)
    return [slice(r, r + sub) for r in range(0, n, sub)]


def _pipeline(stages, n):
    for step in range(n + len(stages) - 1):
        for s, fn in enumerate(stages):
            if 0 <= step - s < n:
                fn(step - s)


def _hidden(x, mod_ref, g_ref):
    gs = g_ref[...] * (1.0 + _mod(mod_ref, 1))
    return (x * lax.rsqrt(jnp.mean(x * x, axis=-1, keepdims=True) + EPS) * gs + _mod(mod_ref, 0)).astype(BF16)


def _even_in_kernel(rope, x_ref, mod_ref, g_ref, w_ref, wuq_ref, wukk_ref, wukv_ref, qn_ref, kvn_ref, *rest):
    ropes, (q_ref, k_ref, v_ref) = rest[:-3], rest[-3:]
    blocks = _sub_blocks(x_ref)
    held = [{} for _ in blocks]

    def rotary(i):
        if not rope:
            return (lambda t: t), (lambda t: t)
        ca, sa, cb, sb = [r[blocks[i], :] for r in ropes]
        return (lambda t: _rope_swap(t, ca, sa)), (lambda t: _rope(t, cb, sb, 8))

    def normalise(i):
        held[i]["h"] = _hidden(x_ref[blocks[i], :], mod_ref, g_ref)

    def project(i):
        rows, h, (rope_a, rope_b) = blocks[i], held[i].pop("h"), rotary(i)

        def proj(c0, n):
            return jnp.dot(h, w_ref[:, c0 * LANES:(c0 + n) * LANES], preferred_element_type=F32)

        for g2 in range(2):
            p = proj(2 * g2, 2)
            for j in range(2):
                t = p[:, j * LANES:(j + 1) * LANES]
                q_ref[rows, (2 * g2 + j) * LANES:(2 * g2 + j + 1) * LANES] = (rope_a(t) * SCALE_64).astype(BF16)
        p = proj(4, 2)
        k_ref[rows, 0:LANES] = rope_a(p[:, :LANES]).astype(BF16)
        v_ref[rows, 0:LANES] = p[:, LANES:].astype(BF16)
        held[i]["cq"] = _rms(proj(6, 2), qn_ref[...]).astype(BF16)
        p = proj(8, 2)
        held[i]["ckv"] = _rms(p[:, :LANES], kvn_ref[...]).astype(BF16)
        held[i]["kpe"] = rope_b(p[:, LANES:])

    def expand(i):
        rows, (_, rope_b) = blocks[i], rotary(i)
        cq, ckv, kpe = held[i].pop("cq"), held[i].pop("ckv"), held[i].pop("kpe")
        for g2 in range(4):
            p = jnp.dot(cq, wuq_ref[:, 2 * g2 * LANES:(2 * g2 + 2) * LANES], preferred_element_type=F32)
            for j in range(2):
                t = p[:, j * LANES:(j + 1) * LANES]
                hd = 4 + 2 * g2 + j
                q_ref[rows, hd * LANES:(hd + 1) * LANES] = (rope_b(t) * SCALE_B).astype(BF16)
        for g2 in range(4):
            kn = jnp.dot(ckv, wukk_ref[:, 2 * g2 * LANES:(2 * g2 + 2) * LANES], preferred_element_type=F32)
            for j in range(2):
                hd = 1 + 2 * g2 + j
                k_ref[rows, hd * LANES:(hd + 1) * LANES] = (kn[:, j * LANES:(j + 1) * LANES] + kpe).astype(BF16)
        for g2 in range(2):
            vb = jnp.dot(ckv, wukv_ref[:, 2 * g2 * LANES:(2 * g2 + 2) * LANES], preferred_element_type=F32)
            v_ref[rows, (1 + 2 * g2) * LANES:(3 + 2 * g2) * LANES] = vb.astype(BF16)

    _pipeline((normalise, project, expand), len(blocks))


def _odd_in_kernel(rope, x_ref, mod_ref, g_ref, w_ref, qkn_ref, *rest):
    ropes, (q_ref, k_ref, v_ref) = rest[:-3], rest[-3:]
    gq, gk = qkn_ref[0:1, :], qkn_ref[1:2, :]
    blocks = _sub_blocks(x_ref)
    held = {}

    def normalise(i):
        held[i] = _hidden(x_ref[blocks[i], :], mod_ref, g_ref)

    def project(i):
        rows, h = blocks[i], held.pop(i)
        if rope:
            ca, sa = [r[rows, :] for r in ropes]
            rope_a = lambda t: _rope_swap(t, ca, sa)
        else:
            rope_a = lambda t: t

        def proj(c0):
            return jnp.dot(h, w_ref[:, c0 * LANES:(c0 + 2) * LANES], preferred_element_type=F32)

        def tiles(p):
            return p[:, :LANES], p[:, LANES:]

        for g2 in range(2):
            for j, t in enumerate(tiles(proj(2 * g2))):
                q_ref[rows, (2 * g2 + j) * LANES:(2 * g2 + j + 1) * LANES] = (
                    rope_a(_head_norm(t, gq)) * SCALE_64).astype(BF16)
        kc, vc = tiles(proj(4))
        k_ref[rows, 0:LANES] = rope_a(_head_norm(kc, gk)).astype(BF16)
        v_ref[rows, 0:LANES] = vc.astype(BF16)
        for g2 in range(2):
            for j, t in enumerate(tiles(proj(6 + 2 * g2))):
                n = 4 + 2 * g2 + j
                q_ref[rows, n * LANES:(n + 1) * LANES] = (rope_a(t) * SCALE_64).astype(BF16)
            for j, t in enumerate(tiles(proj(10 + 2 * g2))):
                n = 1 + 2 * g2 + j
                k_ref[rows, n * LANES:(n + 1) * LANES] = rope_a(t).astype(BF16)
            v_ref[rows, (1 + 2 * g2) * LANES:(3 + 2 * g2) * LANES] = proj(14 + 2 * g2).astype(BF16)

    _pipeline((normalise, project), len(blocks))


def _rows_spec(tile, width):
    return pl.BlockSpec((None, tile, width), lambda b, t: (b, t, 0))


def _const_spec(shape, single=False):
    nd = len(shape)
    kw = dict(pipeline_mode=pl.Buffered(1)) if single else {}
    return pl.BlockSpec(shape, lambda b, t: (0,) * nd, **kw)


def _mod_spec():
    return pl.BlockSpec((None, 1, 6 * D_MODEL), lambda b, t: (b, 0, 0))


def _dense_tile(n):
    return min(DENSE_TOK, n)


def _in_proj(kernel, name, x, mod, gain, weights, ropes, widths):
    bsz, n, _ = x.shape
    tile = _dense_tile(n)
    ins = [x, mod, gain] + list(weights) + list(ropes)
    in_specs = ([_rows_spec(tile, D_MODEL), _mod_spec(), _const_spec((1, D_MODEL))]
                + [_const_spec(w.shape) for w in weights]
                + [pl.BlockSpec((tile, LANES), lambda b, t: (t, 0)) for _ in ropes])
    return pl.pallas_call(
        functools.partial(kernel, bool(ropes)),
        out_shape=[jax.ShapeDtypeStruct((bsz, n, w), BF16) for w in widths],
        grid=(bsz, n // tile),
        in_specs=in_specs,
        out_specs=[_rows_spec(tile, w) for w in widths],
        name=name,
    )(*ins)


def _half(shape, which):
    lane = _lane(shape)
    return lane < HEAD_DIM if which == "lo" else lane >= HEAD_DIM


def _query_head(q, which):
    if which is None:
        return q
    first = _first_head(q.shape)
    return jnp.where(first if which == "lo" else jnp.logical_not(first), q, jnp.zeros_like(q))


def _with_ones(v, which):
    return v if which is None else jnp.where(_half(v.shape, which), v, jnp.ones_like(v))


def _scores(q, k):
    return lax.dot_general(q, k, (((1,), (1,)), ((), ())), preferred_element_type=F32)


def _pair(acc_lo, acc_hi):
    lo = _half(acc_lo.shape, "lo")
    dens = pltpu.roll(jnp.where(lo, acc_hi, acc_lo), LANES // 2, 1)
    return jnp.where(lo, acc_lo, acc_hi) / dens


def _run_groups(groups, q_ref, guard_ref, s_ref, p_ref, acc_ref, l_ref):
    take = lambda ref, rows, t: ref[:, t * LANES:(t + 1) * LANES] if rows is None else ref[rows, t * LANES:(t + 1) * LANES]
    sizes = [[k.shape[0] if rows is None else rows.size for k, _, rows, _ in group[1]] for group in groups]
    base = [sum(len(group[0]) for group in groups[:g]) for g in range(len(groups))]
    items = [(g, u) for g, group in enumerate(groups) for u in range(len(group[0]))]
    slot = {item: i % s_ref.shape[0] for i, item in enumerate(items)}

    @pl.when(guard_ref[0] != 0)
    def _():
        values, tops = {}, {}

        def score(g, u):
            units, parts, swapped = groups[g][:3]
            qt, qh, kt, _, _ = units[u]
            q = _tile(q_ref, qt)
            q = _query_head(q, qh) if swapped else q
            off = 0
            for (k_ref, _, rows, bias), size in zip(parts, sizes[g]):
                s = _scores(q, take(k_ref, rows, kt))
                s_ref[slot[g, u], :, off:off + size] = s if bias is None else s + bias[...]
                off += size

        def rowmax(g, u):
            n = base[g] + u
            tops[n] = []
            for r in range(0, TOK, SOFTMAX_ROWS):
                top = None
                for c in range(sum(sizes[g]) // TOK):
                    sc = s_ref[slot[g, u], r:r + SOFTMAX_ROWS, c * TOK:(c + 1) * TOK]
                    mc = jnp.maximum(sc[:, :LANES], sc[:, LANES:])
                    top = mc if top is None else jnp.maximum(top, mc)
                m = jnp.max(top, axis=-1, keepdims=True)
                tops[n].append(m if groups[g][3] is None else jnp.maximum(m, groups[g][3][u]))

        def exps(g, u):
            n = base[g] + u
            vh, sinks = groups[g][0][u][4], groups[g][3]
            for r, m in zip(range(0, TOK, SOFTMAX_ROWS), tops.pop(n)):
                rows, den = slice(r, r + SOFTMAX_ROWS), None
                for c in range(sum(sizes[g]) // TOK):
                    sc = s_ref[slot[g, u], rows, c * TOK:(c + 1) * TOK]
                    p0, p1 = jnp.exp2(sc[:, :LANES] - m), jnp.exp2(sc[:, LANES:] - m)
                    p_ref[slot[g, u], rows, c * TOK:(c + 1) * TOK] = jnp.concatenate([p0, p1], axis=1).astype(BF16)
                    if vh is None:
                        den = p0 + p1 if den is None else den + (p0 + p1)
                if vh is None:
                    l_ref[n, rows] = jnp.broadcast_to(jnp.sum(den, axis=-1, keepdims=True), (SOFTMAX_ROWS, LANES))
                elif sinks is not None:
                    l_ref[n, rows] = jnp.broadcast_to(jnp.exp2(sinks[u] - m), (SOFTMAX_ROWS, LANES))

        def weigh(g, u):
            n = base[g] + u
            _, _, _, vt, vh = groups[g][0][u]
            acc, off = None, 0
            for p, ((_, v_ref, rows, _), size) in enumerate(zip(groups[g][1], sizes[g])):
                if (g, p, vt, vh) not in values:
                    values[(g, p, vt, vh)] = _with_ones(take(v_ref, rows, vt), vh)
                part = jnp.dot(p_ref[slot[g, u], :, off:off + size], values[(g, p, vt, vh)],
                               preferred_element_type=F32)
                acc = part if acc is None else acc + part
                off += size
            if groups[g][3] is not None:
                acc = acc + jnp.where(_half(acc.shape, vh), 0.0, l_ref[n])
            acc_ref[n] = acc

        _pipeline([lambda i, fn=fn: fn(*items[i]) for fn in (score, rowmax, exps, weigh)], len(items))

    return [[acc_ref[base[g] + u] if unit[4] is not None else acc_ref[base[g] + u] / l_ref[base[g] + u]
             for u, unit in enumerate(group[0])] for g, group in enumerate(groups)]


def _key_parts(kv, self_only):
    k_lat, v_lat, k_ctx, v_ctx = kv
    return ([] if self_only else [(k_lat, v_lat, None, None)]) + [(k_ctx, v_ctx, None, None)]


def _even_attn_kernel(self_only, guard_ref, sink_ref, q_ref, *rest):
    kv, (o_ref, bias_ref, s_ref, p_ref, acc_ref, l_ref) = _kv_refs(self_only, rest)
    scratch = (s_ref, p_ref, acc_ref, l_ref)

    parts = [(kv[2], kv[3], None, None)]
    if not self_only:
        t = pl.program_id(1)
        start = pl.multiple_of(jnp.clip(t * TOK - WINDOW, 0, kv[0].shape[0] - 2 * TOK), LANES)
        qpos = t * TOK + lax.broadcasted_iota(jnp.int32, (TOK, 2 * TOK), 0)
        kpos = start + lax.broadcasted_iota(jnp.int32, (TOK, 2 * TOK), 1)
        bias_ref[...] = jnp.where(jnp.abs(qpos - kpos) <= WINDOW, 0.0, NEG_INF).astype(F32)
        parts.append((kv[0], kv[1], pl.ds(start, 2 * TOK), bias_ref))
    a_units = [(j, h, 0, 0, h) for j in range(4) for h in ("lo", "hi")]
    sinks = [jnp.full((SOFTMAX_ROWS, 1), sink_ref[j if h == "lo" else 4 + j], F32) * LOG2E
             for j, h, _, _, _ in a_units]
    b_units = [(4 + h, None, 1 + h, 1 + h // 2, "lo" if h % 2 == 0 else "hi") for h in range(B_HEADS)]
    groups = [(a_units, parts, True, sinks), (b_units, _key_parts(kv, self_only), False, None)]
    o = _run_groups(groups, q_ref, guard_ref, *scratch)
    for g in range(2):
        for j in range(4):
            o_ref[:, (4 * g + j) * LANES:(4 * g + j + 1) * LANES] = _pair(o[g][2 * j], o[g][2 * j + 1]).astype(BF16)


def _odd_attn_kernel(lam_init, self_only, guard_ref, lam_ref, sub_ref, q_ref, *rest):
    kv, (o_ref, s_ref, p_ref, acc_ref, l_ref) = _kv_refs(self_only, rest)
    lp = lam_ref[...]
    lam = (jnp.exp(jnp.sum(lp[0:1] * lp[1:2], axis=-1, keepdims=True))
           - jnp.exp(jnp.sum(lp[2:3] * lp[3:4], axis=-1, keepdims=True)) + lam_init)

    keys = _key_parts(kv, self_only)
    c_units = [(j, h, 0, 0, h) for j in range(4) for h in ("lo", "hi")]
    d_units = [(4 + j, h, 1 + j, 1 + j, None) for j in range(D_HEADS) for h in ("lo", "hi")]
    od, oc = _run_groups([(d_units, keys, True, None), (c_units, keys, True, None)], q_ref, guard_ref,
                         s_ref, p_ref, acc_ref, l_ref)
    for j in range(4):
        o_ref[:, j * LANES:(j + 1) * LANES] = _pair(oc[2 * j], oc[2 * j + 1]).astype(BF16)
    for j in range(D_HEADS):
        diff = od[2 * j] - lam * od[2 * j + 1]
        o_ref[:, (4 + j) * LANES:(5 + j) * LANES] = (
            _rms(diff, sub_ref[...]) * (1.0 - lam_init)).astype(BF16)


def _kv_refs(self_only, refs):
    if self_only:
        return (None, None, refs[0], refs[1]), refs[2:]
    return tuple(refs[:4]), refs[4:]


def _attention(kernel, name, small, q, k_lat, v_lat, k_ctx, v_ctx, scratch=()):
    bsz, nq, qw = q.shape
    self_only = k_lat is None
    kv_spec = lambda a: pl.BlockSpec((None,) + a.shape[1:], lambda b, t: (b, 0, 0))
    kvs = [k_ctx, v_ctx] if self_only else [k_lat, v_lat, k_ctx, v_ctx]
    nk = sum(a.shape[1] for a in kvs[::2])
    return pl.pallas_call(
        functools.partial(kernel, self_only),
        out_shape=jax.ShapeDtypeStruct((bsz, nq, D_MODEL), BF16),
        grid=(bsz, nq // TOK),
        in_specs=([pl.BlockSpec(memory_space=pltpu.SMEM)] + list(small[1]) + [_rows_spec(TOK, qw)]
                  + [kv_spec(a) for a in kvs]),
        out_specs=_rows_spec(TOK, D_MODEL),
        scratch_shapes=list(scratch) + [pltpu.VMEM((RING, TOK, nk), F32), pltpu.VMEM((RING, TOK, nk), BF16),
                                        pltpu.VMEM((UNITS, TOK, LANES), F32), pltpu.VMEM((UNITS, TOK, LANES), F32)],
        name=name + ("_ctx" if self_only else ""),
    )(jnp.ones((1,), jnp.int32), *small[0], q, *kvs)


def _mix_ffn_kernel(o_ref, wm_ref, x_ref, mod_ref, g_ref, wi_ref, wo_ref, out_ref, h_ref, acc_ref):
    gg = g_ref[0:1, :] * _mod(mod_ref, 2)
    blocks = _sub_blocks(x_ref)
    ys = {}

    def project(i):
        ys[i] = jnp.dot(o_ref[blocks[i], :], wm_ref[...], preferred_element_type=F32)

    def finish(i):
        rows, y = blocks[i], ys.pop(i)
        x1 = x_ref[rows, :] + y * lax.rsqrt(jnp.mean(y * y, axis=-1, keepdims=True) + EPS) * gg
        out_ref[rows, :] = x1
        h_ref[rows, :] = _hidden(x1, mod_ref.at[:, 3 * D_MODEL:5 * D_MODEL], g_ref.at[1:2, :])

    _pipeline((project, finish), len(blocks))

    h = h_ref[...]
    for c in range(FFN_HIDDEN // FFN_CHUNK):
        c0 = c * FFN_CHUNK
        g = jnp.dot(h, wi_ref[:, c0:c0 + FFN_CHUNK], preferred_element_type=F32)
        u = jnp.dot(h, wi_ref[:, FFN_HIDDEN + c0:FFN_HIDDEN + c0 + FFN_CHUNK], preferred_element_type=F32)
        a = (g * jax.nn.sigmoid(g) * u).astype(BF16)
        part = jnp.dot(a, wo_ref[c0:c0 + FFN_CHUNK, :], preferred_element_type=F32)
        if c == 0:
            acc_ref[...] = part
        else:
            acc_ref[...] += part
    gg = g_ref[2:3, :] * _mod(mod_ref, 5)
    for rows in blocks:
        f = acc_ref[rows, :]
        out_ref[rows, :] = out_ref[rows, :] + f * lax.rsqrt(jnp.mean(f * f, axis=-1, keepdims=True) + EPS) * gg


def _mix_ffn(o, w_mix, x, mod, gains, w_in, w_out):
    bsz, n, _ = o.shape
    tile = _dense_tile(n)
    return pl.pallas_call(
        _mix_ffn_kernel,
        out_shape=jax.ShapeDtypeStruct((bsz, n, D_MODEL), F32),
        grid=(bsz, n // tile),
        in_specs=[_rows_spec(tile, D_MODEL), _const_spec(w_mix.shape, single=True), _rows_spec(tile, D_MODEL),
                  _mod_spec(), _const_spec(gains.shape), _const_spec(w_in.shape, single=True),
                  _const_spec(w_out.shape, single=True)],
        out_specs=_rows_spec(tile, D_MODEL),
        scratch_shapes=[pltpu.VMEM((tile, D_MODEL), BF16), pltpu.VMEM((tile, D_MODEL), F32)],
        name="mix_swiglu",
    )(o, w_mix, x, mod, gains, w_in, w_out)


def _rope_tables(seq):
    pos = jnp.arange(seq)
    rows = (pos // GRID_W).astype(F32)[:, None]
    cols = (pos % GRID_W).astype(F32)[:, None]
    lane = jnp.arange(LANES)

    def table(active, is_col, freq_idx, first, half):
        freqs = ROPE_THETA ** (-jnp.arange(half, dtype=F32) / half)
        ang = jnp.where(is_col[None, :], cols, rows) * freqs[freq_idx][None, :]
        cos = jnp.where(active[None, :], jnp.cos(ang), 1.0)
        sin = jnp.where(active[None, :], jnp.sin(ang), 0.0) * jnp.where(first, -1.0, 1.0)[None, :]
        return cos, sin

    ca, sa = table(lane >= 0, (lane % 32) >= 16, lane % 16, lane < HEAD_DIM, 16)
    e = lane - B_NOPE
    cb, sb = table((e >= 0) & (e < B_ROPE), (e // 16) == 1, e % 8, (e % 16) < 8, 8)
    return ca, sa, cb, sb


def _swap_layout(w):
    k, n = w.shape
    return w.reshape(k, n // LANES, 2, 2, 2, 16).transpose(0, 1, 4, 2, 3, 5).reshape(k, n)


def _pair_cols(w):
    k = w.shape[0]
    return w.reshape(k, 2, 4, HEAD_DIM).transpose(0, 2, 1, 3).reshape(k, 8 * HEAD_DIM)


def _pair_rows(w):
    n = w.shape[1]
    return w.reshape(2, 4, HEAD_DIM, n).transpose(1, 0, 2, 3).reshape(8 * HEAD_DIM, n)


def _even_weights(w_in, w_uq, w_ukv, w_out):
    k = w_in.shape[0]
    z = lambda r, n: jnp.zeros((r, n), w_in.dtype)
    w = jnp.concatenate([_swap_layout(jnp.concatenate([_pair_cols(w_in[:, :512]), w_in[:, 512:640]], axis=1)),
                         w_in[:, 640:1152],
                         z(k, B_NOPE), w_in[:, 1152:1184], z(k, LANES - B_NOPE - B_ROPE)], axis=1)
    uq = w_uq.reshape(B_Q_RANK, B_HEADS, B_NOPE + B_ROPE)
    uq = jnp.pad(uq, ((0, 0), (0, 0), (0, LANES - B_NOPE - B_ROPE))).reshape(B_Q_RANK, B_HEADS * LANES)
    ukv = w_ukv.reshape(B_KV_RANK, B_HEADS, B_NOPE + B_V)
    ukk = jnp.pad(ukv[:, :, :B_NOPE], ((0, 0), (0, 0), (0, LANES - B_NOPE))).reshape(B_KV_RANK, B_HEADS * LANES)
    ukvv = ukv[:, :, B_NOPE:].reshape(B_KV_RANK, B_HEADS * B_V)
    wo = jnp.concatenate([_pair_rows(w_out[:512]), w_out[512:]], axis=0)
    return [a.astype(BF16) for a in (w, uq, ukk, ukvv, wo)]


def _odd_weights(w_in, w_out):
    qc_kc = _swap_layout(jnp.concatenate([_pair_cols(w_in[:, :512]), w_in[:, 512:640]], axis=1))
    w = jnp.concatenate([qc_kc, w_in[:, 640:768], _swap_layout(w_in[:, 768:1792]), w_in[:, 1792:]], axis=1)
    wo = jnp.concatenate([_pair_rows(w_out[:512]), w_out[512:]], axis=0)
    return w.astype(BF16), wo.astype(BF16)


def kernel(x, c, ctx, c_ctx, ada_w, ada_b, norm_g, ffn_w_in, ffn_w_out, ev_w_in, ev_sink, ev_q_norm, ev_w_uq,
           ev_kv_norm, ev_w_ukv, ev_w_out, od_w_in, od_qk_norm, od_lambda, od_subln, od_w_out):
    bsz, seq, d = x.shape
    ctx_len = ctx.shape[1]
    assert d == D_MODEL and ctx_len == TOK and seq % DENSE_TOK == 0 and seq % GRID_W == 0

    mod_rows = 8 * (-(-(bsz + 1) // 8))
    s_rows = jnp.zeros((mod_rows, d), F32).at[:bsz].set(c).at[bsz].set(c_ctx)
    mod_all = _ada(s_rows, ada_w, ada_b)

    ropes = _rope_tables(seq)
    xl, xc = x, ctx.reshape(1, bsz * ctx_len, d)
    per_sample = lambda a: a.reshape(bsz, ctx_len, a.shape[-1])
    for l in range(DEPTH):
        last = l == DEPTH - 1
        mod_l, mod_c = mod_all[l, :bsz].reshape(bsz, 1, 6 * d), mod_all[l, bsz].reshape(1, 1, 6 * d)
        gains = [norm_g[l, i].reshape(1, d) for i in range(4)]
        if l % 2 == 0:
            e = l // 2
            w, uq, ukk, ukvv, wo = _even_weights(ev_w_in[e], ev_w_uq[e], ev_w_ukv[e], ev_w_out[e])
            weights = [w, uq, ukk, ukvv, ev_q_norm[e].reshape(1, -1), ev_kv_norm[e].reshape(1, -1)]
            widths = (12 * LANES, 9 * LANES, 5 * LANES)
            ql, kl, vl = _in_proj(_even_in_kernel, "even_in_proj", xl, mod_l, gains[0], weights, ropes, widths)
            qc, kc, vc = [per_sample(a) for a in
                          _in_proj(_even_in_kernel, "even_in_proj_ctx", xc, mod_c, gains[0], weights, (), widths)]
            small = ([ev_sink[e]], [pl.BlockSpec(memory_space=pltpu.SMEM)])
            attn = functools.partial(_attention, _even_attn_kernel, "even_attention", small,
                                     scratch=[pltpu.VMEM((TOK, 2 * TOK), F32)])
        else:
            i = l // 2
            w, wo = _odd_weights(od_w_in[i], od_w_out[i])
            g64 = od_qk_norm[i].reshape(2, 2, 2, 16).transpose(0, 2, 1, 3)
            weights = [w, jnp.broadcast_to(g64[:, :, None], (2, 2, 2, 2, 16)).reshape(2, LANES)]
            widths = (8 * LANES, 5 * LANES, 5 * LANES)
            ql, kl, vl = _in_proj(_odd_in_kernel, "odd_in_proj", xl, mod_l, gains[0], weights, ropes[:2], widths)
            qc, kc, vc = [per_sample(a) for a in
                          _in_proj(_odd_in_kernel, "odd_in_proj_ctx", xc, mod_c, gains[0], weights, (), widths)]
            lam_init = 0.8 - 0.6 * math.exp(-0.3 * l)
            small = ([od_lambda[i], od_subln[i].reshape(1, -1)],
                     [_const_spec((4, HEAD_DIM)), _const_spec((1, LANES))])
            attn = functools.partial(_attention, functools.partial(_odd_attn_kernel, lam_init),
                                     "odd_attention", small)
        w_in, w_out = ffn_w_in[l].astype(BF16), ffn_w_out[l].astype(BF16)
        xl = _mix_ffn(attn(ql, kl, vl, kc, vc), wo, xl, mod_l, norm_g[l, 1:], w_in, w_out)
        if not last:
            oc = attn(qc, None, None, kc, vc).reshape(1, bsz * ctx_len, d)
            xc = _mix_ffn(oc, wo, xc, mod_c, norm_g[l, 1:], w_in, w_out)
    return xl
```

```python
import functools
import math

import jax
import jax.numpy as jnp
from jax import lax
from jax.experimental import pallas as pl
from jax.experimental.pallas import tpu as pltpu

F32 = jnp.float32
BF16 = jnp.bfloat16

D_MODEL = 1024
DEPTH = 4
GRID_W = 64
HEAD_DIM = 64
WINDOW = 128
ROPE_THETA = 10000.0
EPS = 1e-6
NEG_INF = -1e30

A_Q_HEADS = 8
B_HEADS = 8
B_Q_RANK = 256
B_KV_RANK = 128
B_NOPE = 64
B_ROPE = 32
B_V = 64
D_HEADS = 4
FFN_HIDDEN = 2816

LANES = 128
TOK = 256
DENSE_TOK = 1024
SUB_TOK = 256
FFN_CHUNK = 256

LOG2E = math.log2(math.e)
SCALE_64 = HEAD_DIM ** -0.5 * LOG2E
SCALE_B = (B_NOPE + B_ROPE) ** -0.5 * LOG2E
UNITS = 16
RING = 8
QUERY_TILES = 1
SOFTMAX_ROWS = 64


def _rms(x, gain):
    ms = jnp.mean(x * x, axis=-1, keepdims=True)
    return x * lax.rsqrt(ms + EPS) * gain


def _lane(shape):
    return lax.broadcasted_iota(jnp.int32, shape, 1)


def _rope(x, cos, sin_signed, half):
    first = (_lane(x.shape) % (2 * half)) < half
    partner = jnp.where(first, pltpu.roll(x, LANES - half, 1), pltpu.roll(x, half, 1))
    return x * cos + partner * sin_signed


def _rope_swap(x, cos, sin_signed):
    return x * cos + pltpu.roll(x, LANES // 2, 1) * sin_signed


def _first_head(shape):
    return (_lane(shape) % HEAD_DIM) < HEAD_DIM // 2


def _head_norm(x, gain):
    lo = _first_head(x.shape)
    x2 = x * x
    s_lo = jnp.sum(jnp.where(lo, x2, 0.0), axis=-1, keepdims=True)
    s_hi = jnp.sum(jnp.where(lo, 0.0, x2), axis=-1, keepdims=True)
    inv = jnp.where(lo, lax.rsqrt(s_lo / HEAD_DIM + EPS), lax.rsqrt(s_hi / HEAD_DIM + EPS))
    return x * inv * gain


def _tile(ref, j, n=1):
    return ref[:, j * LANES:(j + n) * LANES]


def _ada_kernel(s_ref, w_ref, b_ref, o_ref):
    s = s_ref[...]
    a = (s * jax.nn.sigmoid(s)).astype(BF16)
    o_ref[...] = jnp.dot(a, w_ref[...].astype(BF16), preferred_element_type=F32) + b_ref[...]


def _ada(s_rows, ada_w, ada_b):
    rows = s_rows.shape[0]
    nblk = (6 * D_MODEL) // D_MODEL
    return pl.pallas_call(
        _ada_kernel,
        out_shape=jax.ShapeDtypeStruct((DEPTH, rows, 6 * D_MODEL), F32),
        grid=(DEPTH, nblk),
        in_specs=[
            pl.BlockSpec((rows, D_MODEL), lambda l, n: (0, 0)),
            pl.BlockSpec((None, D_MODEL, D_MODEL), lambda l, n: (l, 0, n)),
            pl.BlockSpec((None, 1, D_MODEL), lambda l, n: (l, 0, n)),
        ],
        out_specs=pl.BlockSpec((None, rows, D_MODEL), lambda l, n: (l, 0, n)),
        name="ada_mod",
    )(s_rows, ada_w, ada_b.reshape(DEPTH, 1, 6 * D_MODEL))


def _mod(mod_ref, i):
    return mod_ref[:, i * D_MODEL:(i + 1) * D_MODEL]


def _sub_blocks(ref):
    n = ref.shape[0]
    sub = min(SUB_TOK, n)
    return [slice(r, r + sub) for r in range(0, n, sub)]


def _pipeline(stages, n):
    for step in range(n + len(stages) - 1):
        for s, fn in enumerate(stages):
            if 0 <= step - s < n:
                fn(step - s)


def _hidden(x, mod_ref, g_ref):
    gs = g_ref[...] * (1.0 + _mod(mod_ref, 1))
    return (x * lax.rsqrt(jnp.mean(x * x, axis=-1, keepdims=True) + EPS) * gs + _mod(mod_ref, 0)).astype(BF16)


def _even_in_kernel(rope, x_ref, mod_ref, g_ref, w_ref, wuq_ref, wukk_ref, wukv_ref, qn_ref, kvn_ref, *rest):
    ropes, (q_ref, k_ref, v_ref) = rest[:-3], rest[-3:]
    blocks = _sub_blocks(x_ref)
    held = [{} for _ in blocks]

    def rotary(i):
        if not rope:
            return (lambda t: t), (lambda t: t)
        ca, sa, cb, sb = [r[blocks[i], :] for r in ropes]
        return (lambda t: _rope_swap(t, ca, sa)), (lambda t: _rope(t, cb, sb, 8))

    def normalise(i):
        held[i]["h"] = _hidden(x_ref[blocks[i], :], mod_ref, g_ref)

    def project(i):
        rows, h, (rope_a, rope_b) = blocks[i], held[i].pop("h"), rotary(i)

        def proj(c0, n):
            return jnp.dot(h, w_ref[:, c0 * LANES:(c0 + n) * LANES], preferred_element_type=F32)

        for g2 in range(2):
            p = proj(2 * g2, 2)
            for j in range(2):
                t = p[:, j * LANES:(j + 1) * LANES]
                q_ref[rows, (2 * g2 + j) * LANES:(2 * g2 + j + 1) * LANES] = (rope_a(t) * SCALE_64).astype(BF16)
        p = proj(4, 2)
        k_ref[rows, 0:LANES] = rope_a(p[:, :LANES]).astype(BF16)
        v_ref[rows, 0:LANES] = p[:, LANES:].astype(BF16)
        held[i]["cq"] = _rms(proj(6, 2), qn_ref[...]).astype(BF16)
        p = proj(8, 2)
        held[i]["ckv"] = _rms(p[:, :LANES], kvn_ref[...]).astype(BF16)
        held[i]["kpe"] = rope_b(p[:, LANES:])

    def expand(i):
        rows, (_, rope_b) = blocks[i], rotary(i)
        cq, ckv, kpe = held[i].pop("cq"), held[i].pop("ckv"), held[i].pop("kpe")
        for g2 in range(4):
            p = jnp.dot(cq, wuq_ref[:, 2 * g2 * LANES:(2 * g2 + 2) * LANES], preferred_element_type=F32)
            for j in range(2):
                t = p[:, j * LANES:(j + 1) * LANES]
                hd = 4 + 2 * g2 + j
                q_ref[rows, hd * LANES:(hd + 1) * LANES] = (rope_b(t) * SCALE_B).astype(BF16)
        for g2 in range(4):
            kn = jnp.dot(ckv, wukk_ref[:, 2 * g2 * LANES:(2 * g2 + 2) * LANES], preferred_element_type=F32)
            for j in range(2):
                hd = 1 + 2 * g2 + j
                k_ref[rows, hd * LANES:(hd + 1) * LANES] = (kn[:, j * LANES:(j + 1) * LANES] + kpe).astype(BF16)
        for g2 in range(2):
            vb = jnp.dot(ckv, wukv_ref[:, 2 * g2 * LANES:(2 * g2 + 2) * LANES], preferred_element_type=F32)
            v_ref[rows, (1 + 2 * g2) * LANES:(3 + 2 * g2) * LANES] = vb.astype(BF16)

    _pipeline((normalise, project, expand), len(blocks))


def _odd_in_kernel(rope, x_ref, mod_ref, g_ref, w_ref, qkn_ref, *rest):
    ropes, (q_ref, k_ref, v_ref) = rest[:-3], rest[-3:]
    gq, gk = qkn_ref[0:1, :], qkn_ref[1:2, :]
    blocks = _sub_blocks(x_ref)
    held = {}

    def normalise(i):
        held[i] = _hidden(x_ref[blocks[i], :], mod_ref, g_ref)

    def project(i):
        rows, h = blocks[i], held.pop(i)
        if rope:
            ca, sa = [r[rows, :] for r in ropes]
            rope_a = lambda t: _rope_swap(t, ca, sa)
        else:
            rope_a = lambda t: t

        def proj(c0):
            return jnp.dot(h, w_ref[:, c0 * LANES:(c0 + 2) * LANES], preferred_element_type=F32)

        def tiles(p):
            return p[:, :LANES], p[:, LANES:]

        for g2 in range(2):
            for j, t in enumerate(tiles(proj(2 * g2))):
                q_ref[rows, (2 * g2 + j) * LANES:(2 * g2 + j + 1) * LANES] = (
                    rope_a(_head_norm(t, gq)) * SCALE_64).astype(BF16)
        kc, vc = tiles(proj(4))
        k_ref[rows, 0:LANES] = rope_a(_head_norm(kc, gk)).astype(BF16)
        v_ref[rows, 0:LANES] = vc.astype(BF16)
        for g2 in range(2):
            for j, t in enumerate(tiles(proj(6 + 2 * g2))):
                n = 4 + 2 * g2 + j
                q_ref[rows, n * LANES:(n + 1) * LANES] = (rope_a(t) * SCALE_64).astype(BF16)
            for j, t in enumerate(tiles(proj(10 + 2 * g2))):
                n = 1 + 2 * g2 + j
                k_ref[rows, n * LANES:(n + 1) * LANES] = rope_a(t).astype(BF16)
            v_ref[rows, (1 + 2 * g2) * LANES:(3 + 2 * g2) * LANES] = proj(14 + 2 * g2).astype(BF16)

    _pipeline((normalise, project), len(blocks))


def _rows_spec(tile, width):
    return pl.BlockSpec((None, tile, width), lambda b, t: (b, t, 0))


def _const_spec(shape, single=False):
    nd = len(shape)
    kw = dict(pipeline_mode=pl.Buffered(1)) if single else {}
    return pl.BlockSpec(shape, lambda b, t: (0,) * nd, **kw)


def _mod_spec():
    return pl.BlockSpec((None, 1, 6 * D_MODEL), lambda b, t: (b, 0, 0))


def _dense_tile(n):
    return min(DENSE_TOK, n)


def _in_proj(kernel, name, x, mod, gain, weights, ropes, widths):
    bsz, n, _ = x.shape
    tile = _dense_tile(n)
    ins = [x, mod, gain] + list(weights) + list(ropes)
    in_specs = ([_rows_spec(tile, D_MODEL), _mod_spec(), _const_spec((1, D_MODEL))]
                + [_const_spec(w.shape) for w in weights]
                + [pl.BlockSpec((tile, LANES), lambda b, t: (t, 0)) for _ in ropes])
    return pl.pallas_call(
        functools.partial(kernel, bool(ropes)),
        out_shape=[jax.ShapeDtypeStruct((bsz, n, w), BF16) for w in widths],
        grid=(bsz, n // tile),
        in_specs=in_specs,
        out_specs=[_rows_spec(tile, w) for w in widths],
        name=name,
    )(*ins)


def _half(shape, which):
    lane = _lane(shape)
    return lane < HEAD_DIM if which == "lo" else lane >= HEAD_DIM


def _query_head(q, which):
    if which is None:
        return q
    first = _first_head(q.shape)
    return jnp.where(first if which == "lo" else jnp.logical_not(first), q, jnp.zeros_like(q))


def _with_ones(v, which):
    return v if which is None else jnp.where(_half(v.shape, which), v, jnp.ones_like(v))


def _scores(q, k):
    return lax.dot_general(q, k, (((1,), (1,)), ((), ())), preferred_element_type=F32)


def _pair(acc_lo, acc_hi):
    lo = _half(acc_lo.shape, "lo")
    dens = pltpu.roll(jnp.where(lo, acc_hi, acc_lo), LANES // 2, 1)
    return jnp.where(lo, acc_lo, acc_hi) / dens


def _run_groups(groups, q_ref, guard_ref, s_ref, p_ref, acc_ref, l_ref):
    take = lambda ref, rows, t: ref[:, t * LANES:(t + 1) * LANES] if rows is None else ref[rows, t * LANES:(t + 1) * LANES]
    sizes = [[k.shape[0] if rows is None else rows.size for k, _, rows, _ in group[1]] for group in groups]
    base = [sum(len(group[0]) for group in groups[:g]) for g in range(len(groups))]
    items = [(g, u) for g, group in enumerate(groups) for u in range(len(group[0]))]
    slot = {item: i % s_ref.shape[0] for i, item in enumerate(items)}

    @pl.when(guard_ref[0] != 0)
    def _():
        values, tops = {}, {}

        def score(g, u):
            units, parts, swapped, _, qrows = groups[g]
            qt, qh, kt, _, _ = units[u]
            q = q_ref[qrows, qt * LANES:(qt + 1) * LANES]
            q = _query_head(q, qh) if swapped else q
            off = 0
            for (k_ref, _, rows, bias), size in zip(parts, sizes[g]):
                s = _scores(q, take(k_ref, rows, kt))
                s_ref[slot[g, u], :, off:off + size] = s if bias is None else s + bias[...]
                off += size

        def rowmax(g, u):
            n = base[g] + u
            tops[n] = []
            for r in range(0, TOK, SOFTMAX_ROWS):
                top = None
                for c in range(sum(sizes[g]) // TOK):
                    sc = s_ref[slot[g, u], r:r + SOFTMAX_ROWS, c * TOK:(c + 1) * TOK]
                    mc = jnp.maximum(sc[:, :LANES], sc[:, LANES:])
                    top = mc if top is None else jnp.maximum(top, mc)
                m = jnp.max(top, axis=-1, keepdims=True)
                tops[n].append(m if groups[g][3] is None else jnp.maximum(m, groups[g][3][u]))

        def exps(half, g, u):
            n = base[g] + u
            vh, sinks = groups[g][0][u][4], groups[g][3]
            blocks = list(zip(range(0, TOK, SOFTMAX_ROWS), tops[n]))
            for r, m in blocks[:len(blocks) // 2] if half == 0 else blocks[len(blocks) // 2:]:
                rows, den = slice(r, r + SOFTMAX_ROWS), None
                for c in range(sum(sizes[g]) // TOK):
                    sc = s_ref[slot[g, u], rows, c * TOK:(c + 1) * TOK]
                    p0, p1 = jnp.exp2(sc[:, :LANES] - m), jnp.exp2(sc[:, LANES:] - m)
                    p_ref[slot[g, u], rows, c * TOK:(c + 1) * TOK] = jnp.concatenate([p0, p1], axis=1).astype(BF16)
                    if vh is None:
                        den = p0 + p1 if den is None else den + (p0 + p1)
                if vh is None:
                    l_ref[n, rows] = jnp.broadcast_to(jnp.sum(den, axis=-1, keepdims=True), (SOFTMAX_ROWS, LANES))
                elif sinks is not None:
                    l_ref[n, rows] = jnp.broadcast_to(jnp.exp2(sinks[u] - m), (SOFTMAX_ROWS, LANES))

        def weigh(g, u):
            n = base[g] + u
            _, _, _, vt, vh = groups[g][0][u]
            acc, off = None, 0
            for p, ((_, v_ref, rows, _), size) in enumerate(zip(groups[g][1], sizes[g])):
                key = (id(v_ref), vt, vh) if rows is None else (g, p, vt, vh)
                if key not in values:
                    values[key] = _with_ones(take(v_ref, rows, vt), vh)
                part = jnp.dot(p_ref[slot[g, u], :, off:off + size], values[key], preferred_element_type=F32)
                acc = part if acc is None else acc + part
                off += size
            if groups[g][3] is not None:
                acc = acc + jnp.where(_half(acc.shape, vh), 0.0, l_ref[n])
            acc_ref[n] = acc

        stages = (score, rowmax, functools.partial(exps, 0), functools.partial(exps, 1), weigh)
        _pipeline([lambda i, fn=fn: fn(*items[i]) for fn in stages], len(items))

    return [[acc_ref[base[g] + u] if unit[4] is not None else acc_ref[base[g] + u] / l_ref[base[g] + u]
             for u, unit in enumerate(group[0])] for g, group in enumerate(groups)]


def _key_parts(kv, self_only):
    k_lat, v_lat, k_ctx, v_ctx = kv
    return ([] if self_only else [(k_lat, v_lat, None, None)]) + [(k_ctx, v_ctx, None, None)]


def _query_tiles(q_ref):
    return [slice(r, r + TOK) for r in range(0, q_ref.shape[0], TOK)]


def _even_attn_kernel(self_only, guard_ref, sink_ref, q_ref, *rest):
    kv, (o_ref, bias_ref, s_ref, p_ref, acc_ref, l_ref) = _kv_refs(self_only, rest)
    a_units = [(j, h, 0, 0, h) for j in range(4) for h in ("lo", "hi")]
    sinks = [jnp.full((SOFTMAX_ROWS, 1), sink_ref[j if h == "lo" else 4 + j], F32) * LOG2E
             for j, h, _, _, _ in a_units]
    b_units = [(4 + h, None, 1 + h, 1 + h // 2, "lo" if h % 2 == 0 else "hi") for h in range(B_HEADS)]
    tiles = _query_tiles(q_ref)
    groups = []
    for i, qrows in enumerate(tiles):
        parts = [(kv[2], kv[3], None, None)]
        if not self_only:
            t = pl.program_id(1) * len(tiles) + i
            start = pl.multiple_of(jnp.clip(t * TOK - WINDOW, 0, kv[0].shape[0] - 2 * TOK), LANES)
            qpos = t * TOK + lax.broadcasted_iota(jnp.int32, (TOK, 2 * TOK), 0)
            kpos = start + lax.broadcasted_iota(jnp.int32, (TOK, 2 * TOK), 1)
            bias_ref[i] = jnp.where(jnp.abs(qpos - kpos) <= WINDOW, 0.0, NEG_INF).astype(F32)
            parts.append((kv[0], kv[1], pl.ds(start, 2 * TOK), bias_ref.at[i]))
        groups += [(a_units, parts, True, sinks, qrows), (b_units, _key_parts(kv, self_only), False, None, qrows)]
    o = _run_groups(groups, q_ref, guard_ref, s_ref, p_ref, acc_ref, l_ref)
    for g, group in enumerate(groups):
        for j in range(4):
            o_ref[group[4], (4 * (g % 2) + j) * LANES:(4 * (g % 2) + j + 1) * LANES] = (
                _pair(o[g][2 * j], o[g][2 * j + 1]).astype(BF16))


def _odd_attn_kernel(lam_init, self_only, guard_ref, lam_ref, sub_ref, q_ref, *rest):
    kv, (o_ref, s_ref, p_ref, acc_ref, l_ref) = _kv_refs(self_only, rest)
    lp = lam_ref[...]
    lam = (jnp.exp(jnp.sum(lp[0:1] * lp[1:2], axis=-1, keepdims=True))
           - jnp.exp(jnp.sum(lp[2:3] * lp[3:4], axis=-1, keepdims=True)) + lam_init)

    keys = _key_parts(kv, self_only)
    c_units = [(j, h, 0, 0, h) for j in range(4) for h in ("lo", "hi")]
    d_units = [(4 + j, h, 1 + j, 1 + j, None) for j in range(D_HEADS) for h in ("lo", "hi")]
    groups = [group for qrows in _query_tiles(q_ref)
              for group in ((d_units, keys, True, None, qrows), (c_units, keys, True, None, qrows))]
    o = _run_groups(groups, q_ref, guard_ref, s_ref, p_ref, acc_ref, l_ref)
    for g in range(0, len(groups), 2):
        qrows, od, oc = groups[g][4], o[g], o[g + 1]
        for j in range(4):
            o_ref[qrows, j * LANES:(j + 1) * LANES] = _pair(oc[2 * j], oc[2 * j + 1]).astype(BF16)
        for j in range(D_HEADS):
            diff = od[2 * j] - lam * od[2 * j + 1]
            o_ref[qrows, (4 + j) * LANES:(5 + j) * LANES] = (
                _rms(diff, sub_ref[...]) * (1.0 - lam_init)).astype(BF16)


def _kv_refs(self_only, refs):
    if self_only:
        return (None, None, refs[0], refs[1]), refs[2:]
    return tuple(refs[:4]), refs[4:]


def _attention(kernel, name, small, q, k_lat, v_lat, k_ctx, v_ctx, window=False):
    bsz, nq, qw = q.shape
    self_only = k_lat is None
    tile = min(QUERY_TILES, nq // TOK) * TOK
    kv_spec = lambda a: pl.BlockSpec((None,) + a.shape[1:], lambda b, t: (b, 0, 0))
    kvs = [k_ctx, v_ctx] if self_only else [k_lat, v_lat, k_ctx, v_ctx]
    nk = sum(a.shape[1] for a in kvs[::2])
    heads = UNITS * tile // TOK
    scratch = [pltpu.VMEM((tile // TOK, TOK, 2 * TOK), F32)] if window else []
    return pl.pallas_call(
        functools.partial(kernel, self_only),
        out_shape=jax.ShapeDtypeStruct((bsz, nq, D_MODEL), BF16),
        grid=(bsz, nq // tile),
        in_specs=([pl.BlockSpec(memory_space=pltpu.SMEM)] + list(small[1]) + [_rows_spec(tile, qw)]
                  + [kv_spec(a) for a in kvs]),
        out_specs=_rows_spec(tile, D_MODEL),
        scratch_shapes=scratch + [pltpu.VMEM((RING, TOK, nk), F32), pltpu.VMEM((RING, TOK, nk), BF16),
                                  pltpu.VMEM((heads, TOK, LANES), F32), pltpu.VMEM((heads, TOK, LANES), F32)],
        name=name + ("_ctx" if self_only else ""),
    )(jnp.ones((1,), jnp.int32), *small[0], q, *kvs)


def _mix_ffn_kernel(o_ref, wm_ref, x_ref, mod_ref, g_ref, wi_ref, wo_ref, out_ref, h_ref, acc_ref):
    gg = g_ref[0:1, :] * _mod(mod_ref, 2)
    blocks = _sub_blocks(x_ref)
    ys = {}

    def project(i):
        ys[i] = jnp.dot(o_ref[blocks[i], :], wm_ref[...], preferred_element_type=F32)

    def finish(i):
        rows, y = blocks[i], ys.pop(i)
        x1 = x_ref[rows, :] + y * lax.rsqrt(jnp.mean(y * y, axis=-1, keepdims=True) + EPS) * gg
        out_ref[rows, :] = x1
        h_ref[rows, :] = _hidden(x1, mod_ref.at[:, 3 * D_MODEL:5 * D_MODEL], g_ref.at[1:2, :])

    _pipeline((project, finish), len(blocks))

    h = h_ref[...]
    for c in range(FFN_HIDDEN // FFN_CHUNK):
        c0 = c * FFN_CHUNK
        g = jnp.dot(h, wi_ref[:, c0:c0 + FFN_CHUNK], preferred_element_type=F32)
        u = jnp.dot(h, wi_ref[:, FFN_HIDDEN + c0:FFN_HIDDEN + c0 + FFN_CHUNK], preferred_element_type=F32)
        a = (g * jax.nn.sigmoid(g) * u).astype(BF16)
        part = jnp.dot(a, wo_ref[c0:c0 + FFN_CHUNK, :], preferred_element_type=F32)
        if c == 0:
            acc_ref[...] = part
        else:
            acc_ref[...] += part
    gg = g_ref[2:3, :] * _mod(mod_ref, 5)
    for rows in blocks:
        f = acc_ref[rows, :]
        out_ref[rows, :] = out_ref[rows, :] + f * lax.rsqrt(jnp.mean(f * f, axis=-1, keepdims=True) + EPS) * gg


def _mix_ffn(o, w_mix, x, mod, gains, w_in, w_out):
    bsz, n, _ = o.shape
    tile = _dense_tile(n)
    return pl.pallas_call(
        _mix_ffn_kernel,
        out_shape=jax.ShapeDtypeStruct((bsz, n, D_MODEL), F32),
        grid=(bsz, n // tile),
        in_specs=[_rows_spec(tile, D_MODEL), _const_spec(w_mix.shape, single=True), _rows_spec(tile, D_MODEL),
                  _mod_spec(), _const_spec(gains.shape), _const_spec(w_in.shape, single=True),
                  _const_spec(w_out.shape, single=True)],
        out_specs=_rows_spec(tile, D_MODEL),
        scratch_shapes=[pltpu.VMEM((tile, D_MODEL), BF16), pltpu.VMEM((tile, D_MODEL), F32)],
        name="mix_swiglu",
    )(o, w_mix, x, mod, gains, w_in, w_out)


def _rope_tables(seq):
    pos = jnp.arange(seq)
    rows = (pos // GRID_W).astype(F32)[:, None]
    cols = (pos % GRID_W).astype(F32)[:, None]
    lane = jnp.arange(LANES)

    def table(active, is_col, freq_idx, first, half):
        freqs = ROPE_THETA ** (-jnp.arange(half, dtype=F32) / half)
        ang = jnp.where(is_col[None, :], cols, rows) * freqs[freq_idx][None, :]
        cos = jnp.where(active[None, :], jnp.cos(ang), 1.0)
        sin = jnp.where(active[None, :], jnp.sin(ang), 0.0) * jnp.where(first, -1.0, 1.0)[None, :]
        return cos, sin

    ca, sa = table(lane >= 0, (lane % 32) >= 16, lane % 16, lane < HEAD_DIM, 16)
    e = lane - B_NOPE
    cb, sb = table((e >= 0) & (e < B_ROPE), (e // 16) == 1, e % 8, (e % 16) < 8, 8)
    return ca, sa, cb, sb


def _swap_layout(w):
    k, n = w.shape
    return w.reshape(k, n // LANES, 2, 2, 2, 16).transpose(0, 1, 4, 2, 3, 5).reshape(k, n)


def _pair_cols(w):
    k = w.shape[0]
    return w.reshape(k, 2, 4, HEAD_DIM).transpose(0, 2, 1, 3).reshape(k, 8 * HEAD_DIM)


def _pair_rows(w):
    n = w.shape[1]
    return w.reshape(2, 4, HEAD_DIM, n).transpose(1, 0, 2, 3).reshape(8 * HEAD_DIM, n)


def _even_weights(w_in, w_uq, w_ukv, w_out):
    k = w_in.shape[0]
    z = lambda r, n: jnp.zeros((r, n), w_in.dtype)
    w = jnp.concatenate([_swap_layout(jnp.concatenate([_pair_cols(w_in[:, :512]), w_in[:, 512:640]], axis=1)),
                         w_in[:, 640:1152],
                         z(k, B_NOPE), w_in[:, 1152:1184], z(k, LANES - B_NOPE - B_ROPE)], axis=1)
    uq = w_uq.reshape(B_Q_RANK, B_HEADS, B_NOPE + B_ROPE)
    uq = jnp.pad(uq, ((0, 0), (0, 0), (0, LANES - B_NOPE - B_ROPE))).reshape(B_Q_RANK, B_HEADS * LANES)
    ukv = w_ukv.reshape(B_KV_RANK, B_HEADS, B_NOPE + B_V)
    ukk = jnp.pad(ukv[:, :, :B_NOPE], ((0, 0), (0, 0), (0, LANES - B_NOPE))).reshape(B_KV_RANK, B_HEADS * LANES)
    ukvv = ukv[:, :, B_NOPE:].reshape(B_KV_RANK, B_HEADS * B_V)
    wo = jnp.concatenate([_pair_rows(w_out[:512]), w_out[512:]], axis=0)
    return [a.astype(BF16) for a in (w, uq, ukk, ukvv, wo)]


def _odd_weights(w_in, w_out):
    qc_kc = _swap_layout(jnp.concatenate([_pair_cols(w_in[:, :512]), w_in[:, 512:640]], axis=1))
    w = jnp.concatenate([qc_kc, w_in[:, 640:768], _swap_layout(w_in[:, 768:1792]), w_in[:, 1792:]], axis=1)
    wo = jnp.concatenate([_pair_rows(w_out[:512]), w_out[512:]], axis=0)
    return w.astype(BF16), wo.astype(BF16)


def kernel(x, c, ctx, c_ctx, ada_w, ada_b, norm_g, ffn_w_in, ffn_w_out, ev_w_in, ev_sink, ev_q_norm, ev_w_uq,
           ev_kv_norm, ev_w_ukv, ev_w_out, od_w_in, od_qk_norm, od_lambda, od_subln, od_w_out):
    bsz, seq, d = x.shape
    ctx_len = ctx.shape[1]
    assert d == D_MODEL and ctx_len == TOK and seq % DENSE_TOK == 0 and seq % GRID_W == 0

    mod_rows = 8 * (-(-(bsz + 1) // 8))
    s_rows = jnp.zeros((mod_rows, d), F32).at[:bsz].set(c).at[bsz].set(c_ctx)
    mod_all = _ada(s_rows, ada_w, ada_b)

    ropes = _rope_tables(seq)
    xl, xc = x, ctx.reshape(1, bsz * ctx_len, d)
    per_sample = lambda a: a.reshape(bsz, ctx_len, a.shape[-1])
    for l in range(DEPTH):
        last = l == DEPTH - 1
        mod_l, mod_c = mod_all[l, :bsz].reshape(bsz, 1, 6 * d), mod_all[l, bsz].reshape(1, 1, 6 * d)
        gains = [norm_g[l, i].reshape(1, d) for i in range(4)]
        if l % 2 == 0:
            e = l // 2
            w, uq, ukk, ukvv, wo = _even_weights(ev_w_in[e], ev_w_uq[e], ev_w_ukv[e], ev_w_out[e])
            weights = [w, uq, ukk, ukvv, ev_q_norm[e].reshape(1, -1), ev_kv_norm[e].reshape(1, -1)]
            widths = (12 * LANES, 9 * LANES, 5 * LANES)
            ql, kl, vl = _in_proj(_even_in_kernel, "even_in_proj", xl, mod_l, gains[0], weights, ropes, widths)
            qc, kc, vc = [per_sample(a) for a in
                          _in_proj(_even_in_kernel, "even_in_proj_ctx", xc, mod_c, gains[0], weights, (), widths)]
            small = ([ev_sink[e]], [pl.BlockSpec(memory_space=pltpu.SMEM)])
            attn = functools.partial(_attention, _even_attn_kernel, "even_attention", small, window=True)
        else:
            i = l // 2
            w, wo = _odd_weights(od_w_in[i], od_w_out[i])
            g64 = od_qk_norm[i].reshape(2, 2, 2, 16).transpose(0, 2, 1, 3)
            weights = [w, jnp.broadcast_to(g64[:, :, None], (2, 2, 2, 2, 16)).reshape(2, LANES)]
            widths = (8 * LANES, 5 * LANES, 5 * LANES)
            ql, kl, vl = _in_proj(_odd_in_kernel, "odd_in_proj", xl, mod_l, gains[0], weights, ropes[:2], widths)
            qc, kc, vc = [per_sample(a) for a in
                          _in_proj(_odd_in_kernel, "odd_in_proj_ctx", xc, mod_c, gains[0], weights, (), widths)]
            lam_init = 0.8 - 0.6 * math.exp(-0.3 * l)
            small = ([od_lambda[i], od_subln[i].reshape(1, -1)],
                     [_const_spec((4, HEAD_DIM)), _const_spec((1, LANES))])
            attn = functools.partial(_attention, functools.partial(_odd_attn_kernel, lam_init),
                                     "odd_attention", small)
        w_in, w_out = ffn_w_in[l].astype(BF16), ffn_w_out[l].astype(BF16)
        xl = _mix_ffn(attn(ql, kl, vl, kc, vc), wo, xl, mod_l, norm_g[l, 1:], w_in, w_out)
        if not last:
            oc = attn(qc, None, None, kc, vc).reshape(1, bsz * ctx_len, d)
            xc = _mix_ffn(oc, wo, xc, mod_c, norm_g[l, 1:], w_in, w_out)
    return xl
```

```python
import functools
import math

import jax
import jax.numpy as jnp
from jax import lax
from jax.experimental import pallas as pl
from jax.experimental.pallas import tpu as pltpu

F32 = jnp.float32
BF16 = jnp.bfloat16

D_MODEL = 1024
DEPTH = 4
GRID_W = 64
HEAD_DIM = 64
WINDOW = 128
ROPE_THETA = 10000.0
EPS = 1e-6
NEG_INF = -1e30

B_HEADS = 8
B_Q_RANK = 256
B_KV_RANK = 128
B_NOPE = 64
B_ROPE = 32
B_V = 64
D_HEADS = 4
FFN_HIDDEN = 2816

LANES = 128
TOK = 256
DENSE_TOK = 1024
SUB_TOK = 256
FFN_CHUNK = 256

LOG2E = math.log2(math.e)
SCALE_64 = HEAD_DIM ** -0.5 * LOG2E
SCALE_B = (B_NOPE + B_ROPE) ** -0.5 * LOG2E
UNITS = 16
RING = 8
QUERY_TILES = 1
SOFTMAX_ROWS = 64


def _rms(x, gain):
    ms = jnp.mean(x * x, axis=-1, keepdims=True)
    return x * lax.rsqrt(ms + EPS) * gain


def _lane(shape):
    return lax.broadcasted_iota(jnp.int32, shape, 1)


def _rope(x, cos, sin_signed, half):
    first = (_lane(x.shape) % (2 * half)) < half
    partner = jnp.where(first, pltpu.roll(x, LANES - half, 1), pltpu.roll(x, half, 1))
    return x * cos + partner * sin_signed


def _rope_swap(x, cos, sin_signed):
    return x * cos + pltpu.roll(x, LANES // 2, 1) * sin_signed


def _first_head(shape):
    return (_lane(shape) % HEAD_DIM) < HEAD_DIM // 2


def _head_norm(x, gain):
    lo = _first_head(x.shape)
    x2 = x * x
    s_lo = jnp.sum(jnp.where(lo, x2, 0.0), axis=-1, keepdims=True)
    s_hi = jnp.sum(jnp.where(lo, 0.0, x2), axis=-1, keepdims=True)
    inv = jnp.where(lo, lax.rsqrt(s_lo / HEAD_DIM + EPS), lax.rsqrt(s_hi / HEAD_DIM + EPS))
    return x * inv * gain


def _ada_kernel(s_ref, w_ref, b_ref, o_ref):
    s = s_ref[...]
    a = (s * jax.nn.sigmoid(s)).astype(BF16)
    o_ref[...] = jnp.dot(a, w_ref[...].astype(BF16), preferred_element_type=F32) + b_ref[...]


def _ada(s_rows, ada_w, ada_b):
    rows = s_rows.shape[0]
    nblk = (6 * D_MODEL) // D_MODEL
    return pl.pallas_call(
        _ada_kernel,
        out_shape=jax.ShapeDtypeStruct((DEPTH, rows, 6 * D_MODEL), F32),
        grid=(DEPTH, nblk),
        in_specs=[
            pl.BlockSpec((rows, D_MODEL), lambda l, n: (0, 0)),
            pl.BlockSpec((None, D_MODEL, D_MODEL), lambda l, n: (l, 0, n)),
            pl.BlockSpec((None, 1, D_MODEL), lambda l, n: (l, 0, n)),
        ],
        out_specs=pl.BlockSpec((None, rows, D_MODEL), lambda l, n: (l, 0, n)),
        name="ada_mod",
    )(s_rows, ada_w, ada_b.reshape(DEPTH, 1, 6 * D_MODEL))


def _mod(mod_ref, i):
    return mod_ref[:, i * D_MODEL:(i + 1) * D_MODEL]


def _sub_blocks(ref):
    n = ref.shape[0]
    sub = min(SUB_TOK, n)
    return [slice(r, r + sub) for r in range(0, n, sub)]


def _pipeline(stages, n):
    for step in range(n + len(stages) - 1):
        for s, fn in enumerate(stages):
            if 0 <= step - s < n:
                fn(step - s)


def _hidden(x, mod_ref, g_ref):
    gs = g_ref[...] * (1.0 + _mod(mod_ref, 1))
    return (x * lax.rsqrt(jnp.mean(x * x, axis=-1, keepdims=True) + EPS) * gs + _mod(mod_ref, 0)).astype(BF16)


def _even_in_kernel(rope, x_ref, mod_ref, g_ref, w_ref, wuq_ref, wukk_ref, wukv_ref, qn_ref, kvn_ref, *rest):
    ropes, (q_ref, k_ref, v_ref) = rest[:-3], rest[-3:]
    blocks = _sub_blocks(x_ref)
    held = [{} for _ in blocks]

    def rotary(i):
        if not rope:
            return (lambda t: t), (lambda t: t)
        ca, sa, cb, sb = [r[blocks[i], :] for r in ropes]
        return (lambda t: _rope_swap(t, ca, sa)), (lambda t: _rope(t, cb, sb, 8))

    def normalise(i):
        held[i]["h"] = _hidden(x_ref[blocks[i], :], mod_ref, g_ref)

    def project(i):
        rows, h, (rope_a, rope_b) = blocks[i], held[i].pop("h"), rotary(i)

        def proj(c0, n):
            return jnp.dot(h, w_ref[:, c0 * LANES:(c0 + n) * LANES], preferred_element_type=F32)

        for g2 in range(2):
            p = proj(2 * g2, 2)
            for j in range(2):
                t = p[:, j * LANES:(j + 1) * LANES]
                q_ref[rows, (2 * g2 + j) * LANES:(2 * g2 + j + 1) * LANES] = (rope_a(t) * SCALE_64).astype(BF16)
        p = proj(4, 2)
        k_ref[rows, 0:LANES] = rope_a(p[:, :LANES]).astype(BF16)
        v_ref[rows, 0:LANES] = p[:, LANES:].astype(BF16)
        held[i]["cq"] = _rms(proj(6, 2), qn_ref[...]).astype(BF16)
        p = proj(8, 2)
        held[i]["ckv"] = _rms(p[:, :LANES], kvn_ref[...]).astype(BF16)
        held[i]["kpe"] = rope_b(p[:, LANES:])

    def expand(i):
        rows, (_, rope_b) = blocks[i], rotary(i)
        cq, ckv, kpe = held[i].pop("cq"), held[i].pop("ckv"), held[i].pop("kpe")
        for g2 in range(4):
            p = jnp.dot(cq, wuq_ref[:, 2 * g2 * LANES:(2 * g2 + 2) * LANES], preferred_element_type=F32)
            for j in range(2):
                t = p[:, j * LANES:(j + 1) * LANES]
                hd = 4 + 2 * g2 + j
                q_ref[rows, hd * LANES:(hd + 1) * LANES] = (rope_b(t) * SCALE_B).astype(BF16)
        for g2 in range(4):
            kn = jnp.dot(ckv, wukk_ref[:, 2 * g2 * LANES:(2 * g2 + 2) * LANES], preferred_element_type=F32)
            for j in range(2):
                hd = 1 + 2 * g2 + j
                k_ref[rows, hd * LANES:(hd + 1) * LANES] = (kn[:, j * LANES:(j + 1) * LANES] + kpe).astype(BF16)
        for g2 in range(2):
            vb = jnp.dot(ckv, wukv_ref[:, 2 * g2 * LANES:(2 * g2 + 2) * LANES], preferred_element_type=F32)
            v_ref[rows, (1 + 2 * g2) * LANES:(3 + 2 * g2) * LANES] = vb.astype(BF16)

    _pipeline((normalise, project, expand), len(blocks))


def _odd_in_kernel(rope, x_ref, mod_ref, g_ref, w_ref, qkn_ref, *rest):
    ropes, (q_ref, k_ref, v_ref) = rest[:-3], rest[-3:]
    gq, gk = qkn_ref[0:1, :], qkn_ref[1:2, :]
    blocks = _sub_blocks(x_ref)
    held = {}

    def normalise(i):
        held[i] = _hidden(x_ref[blocks[i], :], mod_ref, g_ref)

    def project(i):
        rows, h = blocks[i], held.pop(i)
        if rope:
            ca, sa = [r[rows, :] for r in ropes]
            rope_a = lambda t: _rope_swap(t, ca, sa)
        else:
            rope_a = lambda t: t

        def proj(c0):
            return jnp.dot(h, w_ref[:, c0 * LANES:(c0 + 2) * LANES], preferred_element_type=F32)

        def tiles(p):
            return p[:, :LANES], p[:, LANES:]

        for g2 in range(2):
            for j, t in enumerate(tiles(proj(2 * g2))):
                q_ref[rows, (2 * g2 + j) * LANES:(2 * g2 + j + 1) * LANES] = (
                    rope_a(_head_norm(t, gq)) * SCALE_64).astype(BF16)
        kc, vc = tiles(proj(4))
        k_ref[rows, 0:LANES] = rope_a(_head_norm(kc, gk)).astype(BF16)
        v_ref[rows, 0:LANES] = vc.astype(BF16)
        for g2 in range(2):
            for j, t in enumerate(tiles(proj(6 + 2 * g2))):
                n = 4 + 2 * g2 + j
                q_ref[rows, n * LANES:(n + 1) * LANES] = (rope_a(t) * SCALE_64).astype(BF16)
            for j, t in enumerate(tiles(proj(10 + 2 * g2))):
                n = 1 + 2 * g2 + j
                k_ref[rows, n * LANES:(n + 1) * LANES] = rope_a(t).astype(BF16)
            v_ref[rows, (1 + 2 * g2) * LANES:(3 + 2 * g2) * LANES] = proj(14 + 2 * g2).astype(BF16)

    _pipeline((normalise, project), len(blocks))


def _rows_spec(tile, width):
    return pl.BlockSpec((None, tile, width), lambda b, t: (b, t, 0))


def _const_spec(shape, single=False):
    nd = len(shape)
    kw = dict(pipeline_mode=pl.Buffered(1)) if single else {}
    return pl.BlockSpec(shape, lambda b, t: (0,) * nd, **kw)


def _mod_spec():
    return pl.BlockSpec((None, 1, 6 * D_MODEL), lambda b, t: (b, 0, 0))


def _dense_tile(n):
    return min(DENSE_TOK, n)


def _in_proj(kernel, name, x, mod, gain, weights, ropes, widths):
    bsz, n, _ = x.shape
    tile = _dense_tile(n)
    ins = [x, mod, gain] + list(weights) + list(ropes)
    in_specs = ([_rows_spec(tile, D_MODEL), _mod_spec(), _const_spec((1, D_MODEL))]
                + [_const_spec(w.shape) for w in weights]
                + [pl.BlockSpec((tile, LANES), lambda b, t: (t, 0)) for _ in ropes])
    return pl.pallas_call(
        functools.partial(kernel, bool(ropes)),
        out_shape=[jax.ShapeDtypeStruct((bsz, n, w), BF16) for w in widths],
        grid=(bsz, n // tile),
        in_specs=in_specs,
        out_specs=[_rows_spec(tile, w) for w in widths],
        name=name,
    )(*ins)


def _half(shape, which):
    lane = _lane(shape)
    return lane < HEAD_DIM if which == "lo" else lane >= HEAD_DIM


def _query_head(q, which):
    if which is None:
        return q
    first = _first_head(q.shape)
    return jnp.where(first if which == "lo" else jnp.logical_not(first), q, jnp.zeros_like(q))


def _with_ones(v, which):
    return v if which is None else jnp.where(_half(v.shape, which), v, jnp.ones_like(v))


def _scores(q, k):
    return lax.dot_general(q, k, (((1,), (1,)), ((), ())), preferred_element_type=F32)


def _pair(acc_lo, acc_hi):
    lo = _half(acc_lo.shape, "lo")
    dens = pltpu.roll(jnp.where(lo, acc_hi, acc_lo), LANES // 2, 1)
    return jnp.where(lo, acc_lo, acc_hi) / dens


def _run_groups(groups, q_ref, guard_ref, s_ref, p_ref, acc_ref, l_ref):
    take = lambda ref, rows, t: ref[:, t * LANES:(t + 1) * LANES] if rows is None else ref[rows, t * LANES:(t + 1) * LANES]
    sizes = [[k.shape[0] if rows is None else rows.size for k, _, rows, _ in group[1]] for group in groups]
    base = [sum(len(group[0]) for group in groups[:g]) for g in range(len(groups))]
    items = [(g, u) for g, group in enumerate(groups) for u in range(len(group[0]))]
    slot = {item: i % s_ref.shape[0] for i, item in enumerate(items)}

    @pl.when(guard_ref[0] != 0)
    def _():
        values, tops = {}, {}

        def score(g, u):
            units, parts, swapped, _, qrows = groups[g]
            qt, qh, kt, _, _ = units[u]
            q = q_ref[qrows, qt * LANES:(qt + 1) * LANES]
            q = _query_head(q, qh) if swapped else q
            off = 0
            for (k_ref, _, rows, bias), size in zip(parts, sizes[g]):
                s = _scores(q, take(k_ref, rows, kt))
                s_ref[slot[g, u], :, off:off + size] = s if bias is None else s + bias[...]
                off += size

        def rowmax(g, u):
            n = base[g] + u
            tops[n] = []
            for r in range(0, TOK, SOFTMAX_ROWS):
                top = None
                for c in range(sum(sizes[g]) // TOK):
                    sc = s_ref[slot[g, u], r:r + SOFTMAX_ROWS, c * TOK:(c + 1) * TOK]
                    mc = jnp.maximum(sc[:, :LANES], sc[:, LANES:])
                    top = mc if top is None else jnp.maximum(top, mc)
                m = jnp.max(top, axis=-1, keepdims=True)
                tops[n].append(m if groups[g][3] is None else jnp.maximum(m, groups[g][3][u]))

        def exps(half, g, u):
            n = base[g] + u
            vh, sinks = groups[g][0][u][4], groups[g][3]
            blocks = list(zip(range(0, TOK, SOFTMAX_ROWS), tops[n]))
            for r, m in blocks[:len(blocks) // 2] if half == 0 else blocks[len(blocks) // 2:]:
                rows, den = slice(r, r + SOFTMAX_ROWS), None
                for c in range(sum(sizes[g]) // TOK):
                    sc = s_ref[slot[g, u], rows, c * TOK:(c + 1) * TOK]
                    p0, p1 = jnp.exp2(sc[:, :LANES] - m), jnp.exp2(sc[:, LANES:] - m)
                    p_ref[slot[g, u], rows, c * TOK:(c + 1) * TOK] = jnp.concatenate([p0, p1], axis=1).astype(BF16)
                    if vh is None:
                        den = p0 + p1 if den is None else den + (p0 + p1)
                if vh is None:
                    l_ref[n, rows] = jnp.broadcast_to(jnp.sum(den, axis=-1, keepdims=True), (SOFTMAX_ROWS, LANES))
                elif sinks is not None:
                    l_ref[n, rows] = jnp.broadcast_to(jnp.exp2(sinks[u] - m), (SOFTMAX_ROWS, LANES))

        def weigh(g, u):
            n = base[g] + u
            _, _, _, vt, vh = groups[g][0][u]
            acc, off = None, 0
            for p, ((_, v_ref, rows, _), size) in enumerate(zip(groups[g][1], sizes[g])):
                key = (id(v_ref), vt, vh) if rows is None else (g, p, vt, vh)
                if key not in values:
                    values[key] = _with_ones(take(v_ref, rows, vt), vh)
                part = jnp.dot(p_ref[slot[g, u], :, off:off + size], values[key], preferred_element_type=F32)
                acc = part if acc is None else acc + part
                off += size
            if groups[g][3] is not None:
                acc = acc + jnp.where(_half(acc.shape, vh), 0.0, l_ref[n])
            acc_ref[n] = acc

        stages = (score, rowmax, functools.partial(exps, 0), functools.partial(exps, 1), weigh)
        _pipeline([lambda i, fn=fn: fn(*items[i]) for fn in stages], len(items))

    return [[acc_ref[base[g] + u] if unit[4] is not None else acc_ref[base[g] + u] / l_ref[base[g] + u]
             for u, unit in enumerate(group[0])] for g, group in enumerate(groups)]


def _key_parts(kv, self_only):
    k_lat, v_lat, k_ctx, v_ctx = kv
    return ([] if self_only else [(k_lat, v_lat, None, None)]) + [(k_ctx, v_ctx, None, None)]


def _query_tiles(q_ref):
    return [slice(r, r + TOK) for r in range(0, q_ref.shape[0], TOK)]


def _even_attn_kernel(self_only, guard_ref, sink_ref, q_ref, *rest):
    kv, (o_ref, bias_ref, s_ref, p_ref, acc_ref, l_ref) = _kv_refs(self_only, rest)
    a_units = [(j, h, 0, 0, h) for j in range(4) for h in ("lo", "hi")]
    sinks = [jnp.full((SOFTMAX_ROWS, 1), sink_ref[j if h == "lo" else 4 + j], F32) * LOG2E
             for j, h, _, _, _ in a_units]
    b_units = [(4 + h, None, 1 + h, 1 + h // 2, "lo" if h % 2 == 0 else "hi") for h in range(B_HEADS)]
    tiles = _query_tiles(q_ref)
    groups = []
    for i, qrows in enumerate(tiles):
        parts = [(kv[2], kv[3], None, None)]
        if not self_only:
            t = pl.program_id(1) * len(tiles) + i
            start = pl.multiple_of(jnp.clip(t * TOK - WINDOW, 0, kv[0].shape[0] - 2 * TOK), LANES)
            qpos = t * TOK + lax.broadcasted_iota(jnp.int32, (TOK, 2 * TOK), 0)
            kpos = start + lax.broadcasted_iota(jnp.int32, (TOK, 2 * TOK), 1)
            bias_ref[i] = jnp.where(jnp.abs(qpos - kpos) <= WINDOW, 0.0, NEG_INF).astype(F32)
            parts.append((kv[0], kv[1], pl.ds(start, 2 * TOK), bias_ref.at[i]))
        groups += [(a_units, parts, True, sinks, qrows), (b_units, _key_parts(kv, self_only), False, None, qrows)]
    o = _run_groups(groups, q_ref, guard_ref, s_ref, p_ref, acc_ref, l_ref)
    for g, group in enumerate(groups):
        for j in range(4):
            o_ref[group[4], (4 * (g % 2) + j) * LANES:(4 * (g % 2) + j + 1) * LANES] = (
                _pair(o[g][2 * j], o[g][2 * j + 1]).astype(BF16))


def _odd_attn_kernel(lam_init, self_only, guard_ref, lam_ref, sub_ref, q_ref, *rest):
    kv, (o_ref, s_ref, p_ref, acc_ref, l_ref) = _kv_refs(self_only, rest)
    lp = lam_ref[...]
    lam = (jnp.exp(jnp.sum(lp[0:1] * lp[1:2], axis=-1, keepdims=True))
           - jnp.exp(jnp.sum(lp[2:3] * lp[3:4], axis=-1, keepdims=True)) + lam_init)

    keys = _key_parts(kv, self_only)
    c_units = [(j, h, 0, 0, h) for j in range(4) for h in ("lo", "hi")]
    d_units = [(4 + j, h, 1 + j, 1 + j, None) for j in range(D_HEADS) for h in ("lo", "hi")]
    groups = [group for qrows in _query_tiles(q_ref)
              for group in ((d_units, keys, True, None, qrows), (c_units, keys, True, None, qrows))]
    o = _run_groups(groups, q_ref, guard_ref, s_ref, p_ref, acc_ref, l_ref)
    for g in range(0, len(groups), 2):
        qrows, od, oc = groups[g][4], o[g], o[g + 1]
        for j in range(4):
            o_ref[qrows, j * LANES:(j + 1) * LANES] = _pair(oc[2 * j], oc[2 * j + 1]).astype(BF16)
        for j in range(D_HEADS):
            diff = od[2 * j] - lam * od[2 * j + 1]
            o_ref[qrows, (4 + j) * LANES:(5 + j) * LANES] = (
                _rms(diff, sub_ref[...]) * (1.0 - lam_init)).astype(BF16)


def _kv_refs(self_only, refs):
    if self_only:
        return (None, None, refs[0], refs[1]), refs[2:]
    return tuple(refs[:4]), refs[4:]


def _attention(kernel, name, small, q, k_lat, v_lat, k_ctx, v_ctx, window=False):
    bsz, nq, qw = q.shape
    self_only = k_lat is None
    tile = min(QUERY_TILES, nq // TOK) * TOK
    kv_spec = lambda a: pl.BlockSpec((None,) + a.shape[1:], lambda b, t: (b, 0, 0))
    kvs = [k_ctx, v_ctx] if self_only else [k_lat, v_lat, k_ctx, v_ctx]
    nk = sum(a.shape[1] for a in kvs[::2])
    heads = UNITS * tile // TOK
    scratch = [pltpu.VMEM((tile // TOK, TOK, 2 * TOK), F32)] if window else []
    return pl.pallas_call(
        functools.partial(kernel, self_only),
        out_shape=jax.ShapeDtypeStruct((bsz, nq, D_MODEL), BF16),
        grid=(bsz, nq // tile),
        in_specs=([pl.BlockSpec(memory_space=pltpu.SMEM)] + list(small[1]) + [_rows_spec(tile, qw)]
                  + [kv_spec(a) for a in kvs]),
        out_specs=_rows_spec(tile, D_MODEL),
        scratch_shapes=scratch + [pltpu.VMEM((RING, TOK, nk), F32), pltpu.VMEM((RING, TOK, nk), BF16),
                                  pltpu.VMEM((heads, TOK, LANES), F32), pltpu.VMEM((heads, TOK, LANES), F32)],
        name=name + ("_ctx" if self_only else ""),
    )(jnp.ones((1,), jnp.int32), *small[0], q, *kvs)


def _mix_ffn_kernel(o_ref, wm_ref, x_ref, mod_ref, g_ref, wi_ref, wo_ref, out_ref, h_ref, acc_ref):
    gg = g_ref[0:1, :] * _mod(mod_ref, 2)
    blocks = _sub_blocks(x_ref)
    ys = {}

    def project(i):
        ys[i] = jnp.dot(o_ref[blocks[i], :], wm_ref[...], preferred_element_type=F32)

    def finish(i):
        rows, y = blocks[i], ys.pop(i)
        x1 = x_ref[rows, :] + y * lax.rsqrt(jnp.mean(y * y, axis=-1, keepdims=True) + EPS) * gg
        out_ref[rows, :] = x1
        h_ref[rows, :] = _hidden(x1, mod_ref.at[:, 3 * D_MODEL:5 * D_MODEL], g_ref.at[1:2, :])

    _pipeline((project, finish), len(blocks))

    h = h_ref[...]
    for c in range(FFN_HIDDEN // FFN_CHUNK):
        c0 = c * FFN_CHUNK
        g = jnp.dot(h, wi_ref[:, c0:c0 + FFN_CHUNK], preferred_element_type=F32)
        u = jnp.dot(h, wi_ref[:, FFN_HIDDEN + c0:FFN_HIDDEN + c0 + FFN_CHUNK], preferred_element_type=F32)
        a = (g * jax.nn.sigmoid(g) * u).astype(BF16)
        part = jnp.dot(a, wo_ref[c0:c0 + FFN_CHUNK, :], preferred_element_type=F32)
        if c == 0:
            acc_ref[...] = part
        else:
            acc_ref[...] += part
    gg = g_ref[2:3, :] * _mod(mod_ref, 5)
    for rows in blocks:
        f = acc_ref[rows, :]
        out_ref[rows, :] = out_ref[rows, :] + f * lax.rsqrt(jnp.mean(f * f, axis=-1, keepdims=True) + EPS) * gg


def _mix_ffn(o, w_mix, x, mod, gains, w_in, w_out):
    bsz, n, _ = o.shape
    tile = _dense_tile(n)
    return pl.pallas_call(
        _mix_ffn_kernel,
        out_shape=jax.ShapeDtypeStruct((bsz, n, D_MODEL), F32),
        grid=(bsz, n // tile),
        in_specs=[_rows_spec(tile, D_MODEL), _const_spec(w_mix.shape, single=True), _rows_spec(tile, D_MODEL),
                  _mod_spec(), _const_spec(gains.shape), _const_spec(w_in.shape, single=True),
                  _const_spec(w_out.shape, single=True)],
        out_specs=_rows_spec(tile, D_MODEL),
        scratch_shapes=[pltpu.VMEM((tile, D_MODEL), BF16), pltpu.VMEM((tile, D_MODEL), F32)],
        name="mix_swiglu",
    )(o, w_mix, x, mod, gains, w_in, w_out)


def _rope_tables(seq):
    pos = jnp.arange(seq)
    rows = (pos // GRID_W).astype(F32)[:, None]
    cols = (pos % GRID_W).astype(F32)[:, None]
    lane = jnp.arange(LANES)

    def table(active, is_col, freq_idx, first, half):
        freqs = ROPE_THETA ** (-jnp.arange(half, dtype=F32) / half)
        ang = jnp.where(is_col[None, :], cols, rows) * freqs[freq_idx][None, :]
        cos = jnp.where(active[None, :], jnp.cos(ang), 1.0)
        sin = jnp.where(active[None, :], jnp.sin(ang), 0.0) * jnp.where(first, -1.0, 1.0)[None, :]
        return cos, sin

    ca, sa = table(lane >= 0, (lane % 32) >= 16, lane % 16, lane < HEAD_DIM, 16)
    e = lane - B_NOPE
    cb, sb = table((e >= 0) & (e < B_ROPE), (e // 16) == 1, e % 8, (e % 16) < 8, 8)
    return ca, sa, cb, sb


def _swap_layout(w):
    k, n = w.shape
    return w.reshape(k, n // LANES, 2, 2, 2, 16).transpose(0, 1, 4, 2, 3, 5).reshape(k, n)


def _pair_cols(w):
    k = w.shape[0]
    return w.reshape(k, 2, 4, HEAD_DIM).transpose(0, 2, 1, 3).reshape(k, 8 * HEAD_DIM)


def _pair_rows(w):
    n = w.shape[1]
    return w.reshape(2, 4, HEAD_DIM, n).transpose(1, 0, 2, 3).reshape(8 * HEAD_DIM, n)


def _even_weights(w_in, w_uq, w_ukv, w_out):
    k = w_in.shape[0]
    z = lambda r, n: jnp.zeros((r, n), w_in.dtype)
    w = jnp.concatenate([_swap_layout(jnp.concatenate([_pair_cols(w_in[:, :512]), w_in[:, 512:640]], axis=1)),
                         w_in[:, 640:1152],
                         z(k, B_NOPE), w_in[:, 1152:1184], z(k, LANES - B_NOPE - B_ROPE)], axis=1)
    uq = w_uq.reshape(B_Q_RANK, B_HEADS, B_NOPE + B_ROPE)
    uq = jnp.pad(uq, ((0, 0), (0, 0), (0, LANES - B_NOPE - B_ROPE))).reshape(B_Q_RANK, B_HEADS * LANES)
    ukv = w_ukv.reshape(B_KV_RANK, B_HEADS, B_NOPE + B_V)
    ukk = jnp.pad(ukv[:, :, :B_NOPE], ((0, 0), (0, 0), (0, LANES - B_NOPE))).reshape(B_KV_RANK, B_HEADS * LANES)
    ukvv = ukv[:, :, B_NOPE:].reshape(B_KV_RANK, B_HEADS * B_V)
    wo = jnp.concatenate([_pair_rows(w_out[:512]), w_out[512:]], axis=0)
    return [a.astype(BF16) for a in (w, uq, ukk, ukvv, wo)]


def _odd_weights(w_in, w_out):
    qc_kc = _swap_layout(jnp.concatenate([_pair_cols(w_in[:, :512]), w_in[:, 512:640]], axis=1))
    w = jnp.concatenate([qc_kc, w_in[:, 640:768], _swap_layout(w_in[:, 768:1792]), w_in[:, 1792:]], axis=1)
    wo = jnp.concatenate([_pair_rows(w_out[:512]), w_out[512:]], axis=0)
    return w.astype(BF16), wo.astype(BF16)


def kernel(x, c, ctx, c_ctx, ada_w, ada_b, norm_g, ffn_w_in, ffn_w_out, ev_w_in, ev_sink, ev_q_norm, ev_w_uq,
           ev_kv_norm, ev_w_ukv, ev_w_out, od_w_in, od_qk_norm, od_lambda, od_subln, od_w_out):
    bsz, seq, d = x.shape
    ctx_len = ctx.shape[1]
    assert d == D_MODEL and ctx_len == TOK and seq % DENSE_TOK == 0 and seq % GRID_W == 0

    mod_rows = 8 * (-(-(bsz + 1) // 8))
    s_rows = jnp.zeros((mod_rows, d), F32).at[:bsz].set(c).at[bsz].set(c_ctx)
    mod_all = _ada(s_rows, ada_w, ada_b)

    ropes = _rope_tables(seq)
    xl, xc = x, ctx.reshape(1, bsz * ctx_len, d)
    per_sample = lambda a: a.reshape(bsz, ctx_len, a.shape[-1])
    for l in range(DEPTH):
        last = l == DEPTH - 1
        mod_l, mod_c = mod_all[l, :bsz].reshape(bsz, 1, 6 * d), mod_all[l, bsz].reshape(1, 1, 6 * d)
        gains = [norm_g[l, i].reshape(1, d) for i in range(4)]
        if l % 2 == 0:
            e = l // 2
            w, uq, ukk, ukvv, wo = _even_weights(ev_w_in[e], ev_w_uq[e], ev_w_ukv[e], ev_w_out[e])
            weights = [w, uq, ukk, ukvv, ev_q_norm[e].reshape(1, -1), ev_kv_norm[e].reshape(1, -1)]
            widths = (12 * LANES, 9 * LANES, 5 * LANES)
            ql, kl, vl = _in_proj(_even_in_kernel, "even_in_proj", xl, mod_l, gains[0], weights, ropes, widths)
            qc, kc, vc = [per_sample(a) for a in
                          _in_proj(_even_in_kernel, "even_in_proj_ctx", xc, mod_c, gains[0], weights, (), widths)]
            small = ([ev_sink[e]], [pl.BlockSpec(memory_space=pltpu.SMEM)])
            attn = functools.partial(_attention, _even_attn_kernel, "even_attention", small, window=True)
        else:
            i = l // 2
            w, wo = _odd_weights(od_w_in[i], od_w_out[i])
            g64 = od_qk_norm[i].reshape(2, 2, 2, 16).transpose(0, 2, 1, 3)
            weights = [w, jnp.broadcast_to(g64[:, :, None], (2, 2, 2, 2, 16)).reshape(2, LANES)]
            widths = (8 * LANES, 5 * LANES, 5 * LANES)
            ql, kl, vl = _in_proj(_odd_in_kernel, "odd_in_proj", xl, mod_l, gains[0], weights, ropes[:2], widths)
            qc, kc, vc = [per_sample(a) for a in
                          _in_proj(_odd_in_kernel, "odd_in_proj_ctx", xc, mod_c, gains[0], weights, (), widths)]
            lam_init = 0.8 - 0.6 * math.exp(-0.3 * l)
            small = ([od_lambda[i], od_subln[i].reshape(1, -1)],
                     [_const_spec((4, HEAD_DIM)), _const_spec((1, LANES))])
            attn = functools.partial(_attention, functools.partial(_odd_attn_kernel, lam_init),
                                     "odd_attention", small)
        w_in, w_out = ffn_w_in[l].astype(BF16), ffn_w_out[l].astype(BF16)
        xl = _mix_ffn(attn(ql, kl, vl, kc, vc), wo, xl, mod_l, norm_g[l, 1:], w_in, w_out)
        if not last:
            oc = attn(qc, None, None, kc, vc).reshape(1, bsz * ctx_len, d)
            xc = _mix_ffn(oc, wo, xc, mod_c, norm_g[l, 1:], w_in, w_out)
    return xl
```
